```python
import jax, jax.numpy as jnp
from jax import lax
import numpy as np

D_MODEL = 4096
BATCH = 2
SEQ = 4096
DEPTH = 2

M_HEADS = 8
M_DQK = 256
M_DV = 512
M_CHUNK = 64
GATE_SOFTCAP = 15.0
A_HEADS = 32
A_Q_LORA = 1024
A_KV_LORA = 512
A_NOPE = 128
A_ROPE = 64
A_QK = A_NOPE + A_ROPE
A_DV = 128
Q_BLOCK = 128
ROPE_THETA = 10000.0
D_FF = 4 * D_MODEL
EPS = 1e-6

M_QK_W = M_HEADS * M_DQK
M_V_W = M_HEADS * M_DV
A_V_W = A_HEADS * A_DV
IN_SPLITS = (M_QK_W, M_QK_W, M_V_W, M_V_W, M_HEADS, M_HEADS,
             A_Q_LORA, A_KV_LORA + A_ROPE, D_MODEL, D_MODEL)
IN_COLS = sum(IN_SPLITS)

kernel_name = "hybrid_mlstm_mla_gated_sqrelu"


def rms_norm(x, g):
    xf = x.astype(jnp.float32)
    y = xf * lax.rsqrt(jnp.mean(xf * xf, axis=-1, keepdims=True) + EPS)
    return (y * g.astype(jnp.float32)).astype(x.dtype)


def softcap(x, cap):
    return cap * jnp.tanh(x / cap)


def rotary(x, cos, sin):
    xf = x.astype(jnp.float32)
    x1, x2 = xf[..., : A_ROPE // 2], xf[..., A_ROPE // 2:]
    out = jnp.concatenate([x1 * cos - x2 * sin, x2 * cos + x1 * sin], axis=-1)
    return out.astype(x.dtype)


def mlstm_chunkwise(q, k, v, logi, logf):
    B, S = q.shape[0], q.shape[1]
    nc = S // M_CHUNK

    def chunks4(t):
        t = t.astype(jnp.float32).reshape(B, nc, M_CHUNK, M_HEADS, t.shape[-1])
        return t.transpose(1, 0, 3, 2, 4)

    def chunks3(t):
        return t.astype(jnp.float32).reshape(B, nc, M_CHUNK, M_HEADS).transpose(1, 0, 3, 2)

    qc = chunks4(q) * (M_DQK ** -0.5)
    kc, vc = chunks4(k), chunks4(v)
    lic, lfc = chunks3(logi), chunks3(logf)
    causal = jnp.tril(jnp.ones((M_CHUNK, M_CHUNK), dtype=bool))

    def step(carry, xs):
        C, n, m = carry
        qb, kb, vb, lib, lfb = xs
        b = jnp.cumsum(lfb, axis=-1)
        D = jnp.where(causal, b[..., :, None] - b[..., None, :] + lib[..., None, :], -jnp.inf)
        inter = b + m[..., None]
        mj = jnp.maximum(inter, jnp.max(D, axis=-1))
        w_inter = jnp.exp(inter - mj)
        P = jnp.exp(D - mj[..., None]) * jnp.einsum('bhld,bhrd->bhlr', qb, kb)
        num = (w_inter[..., None] * jnp.einsum('bhld,bhde->bhle', qb, C)
               + jnp.einsum('bhlr,bhre->bhle', P, vb))
        nq = w_inter * jnp.einsum('bhld,bhd->bhl', qb, n) + jnp.sum(P, axis=-1)
        h = num / jnp.maximum(jnp.abs(nq), jnp.exp(-mj))[..., None]
        bL = b[..., -1]
        a = bL[..., None] - b + lib
        m_new = jnp.maximum(bL + m, jnp.max(a, axis=-1))
        decay = jnp.exp(bL + m - m_new)
        wa = jnp.exp(a - m_new[..., None])
        C_new = decay[..., None, None] * C + jnp.einsum('bhl,bhld,bhle->bhde', wa, kb, vb)
        n_new = decay[..., None] * n + jnp.einsum('bhl,bhld->bhd', wa, kb)
        return (C_new, n_new, m_new), h

    init = (jnp.zeros((B, M_HEADS, M_DQK, M_DV), jnp.float32),
            jnp.zeros((B, M_HEADS, M_DQK), jnp.float32),
            jnp.zeros((B, M_HEADS), jnp.float32))
    _, hs = lax.scan(step, init, (qc, kc, vc, lic, lfc))
    return hs.transpose(1, 0, 3, 2, 4).reshape(B, S, M_HEADS, M_DV)


def mla_attention(q, k, v):
    B, S = q.shape[0], q.shape[1]
    nb = S // Q_BLOCK
    qb = q.reshape(B, nb, Q_BLOCK, A_HEADS, A_QK).transpose(1, 0, 2, 3, 4)
    starts = jnp.arange(nb, dtype=jnp.int32) * Q_BLOCK
    k_idx = jnp.arange(S, dtype=jnp.int32)
    scale = A_QK ** -0.5

    def block(args):
        qblk, s0 = args
        s = jnp.einsum('bqhd,bkhd->bhqk', qblk, k, preferred_element_type=jnp.float32) * scale
        q_idx = s0 + jnp.arange(Q_BLOCK, dtype=jnp.int32)
        s = jnp.where(k_idx[None, :] <= q_idx[:, None], s, -jnp.inf)
        p = jax.nn.softmax(s, axis=-1)
        return jnp.einsum('bhqk,bkhd->bqhd', p.astype(v.dtype), v)

    o = lax.map(block, (qb, starts))
    return o.transpose(1, 0, 2, 3, 4).reshape(B, S, A_V_W)


def hybrid_layer(x, cos, sin, norm_mix, w_in, b_igate, b_fgate, m_head_norm, w_out_m,
                 q_a_norm, w_uq, kv_a_norm, w_ukv, qk_norm_q, qk_norm_k, w_out_a,
                 w_out, norm_mlp, w_up, w_down):
    B, S = x.shape[0], x.shape[1]
    h = rms_norm(x, norm_mix)
    proj = h @ w_in
    offsets = np.cumsum(IN_SPLITS)[:-1].tolist()
    q_m, k_m, v_m, o_m, i_m, f_m, c_q, kv_a, g_a, g_b = jnp.split(proj, offsets, axis=-1)

    logi = softcap(i_m.astype(jnp.float32) + b_igate.astype(jnp.float32), GATE_SOFTCAP)
    logf = jax.nn.log_sigmoid(
        softcap(f_m.astype(jnp.float32) + b_fgate.astype(jnp.float32), GATE_SOFTCAP))
    hm = mlstm_chunkwise(q_m.reshape(B, S, M_HEADS, M_DQK),
                         k_m.reshape(B, S, M_HEADS, M_DQK),
                         v_m.reshape(B, S, M_HEADS, M_DV), logi, logf)
    hm = rms_norm(hm, m_head_norm.reshape(M_HEADS, M_DV)).reshape(B, S, M_V_W)
    hm = hm.astype(x.dtype) * jax.nn.sigmoid(o_m)
    y_a = hm @ w_out_m

    q = (rms_norm(c_q, q_a_norm) @ w_uq).reshape(B, S, A_HEADS, A_QK)
    c_kv, k_rope = kv_a[..., :A_KV_LORA], kv_a[..., A_KV_LORA:]
    kv = (rms_norm(c_kv, kv_a_norm) @ w_ukv).reshape(B, S, A_HEADS, A_NOPE + A_DV)
    k_nope, v = kv[..., :A_NOPE], kv[..., A_NOPE:]
    k = jnp.concatenate(
        [k_nope, jnp.broadcast_to(k_rope[:, :, None, :], (B, S, A_HEADS, A_ROPE))], axis=-1)
    q = rms_norm(q, qk_norm_q)
    k = rms_norm(k, qk_norm_k)
    q = jnp.concatenate([q[..., :A_NOPE], rotary(q[..., A_NOPE:], cos, sin)], axis=-1)
    k = jnp.concatenate([k[..., :A_NOPE], rotary(k[..., A_NOPE:], cos, sin)], axis=-1)
    y_b = mla_attention(q, k, v) @ w_out_a

    merged = jax.nn.sigmoid(g_a) * y_a + jax.nn.sigmoid(g_b) * y_b
    x = x + merged @ w_out

    h2 = rms_norm(x, norm_mlp)
    x = x + jnp.square(jax.nn.relu(h2 @ w_up)) @ w_down
    return x


def setup_inputs(seed: int = 0) -> dict:
    key = jax.random.key(seed)
    ks = jax.random.split(key, 20)

    def w(k, shape, fan_in):
        return jax.random.normal(k, shape, jnp.float32) * (fan_in ** -0.5)

    def gain(k, shape):
        return 1.0 + 0.02 * jax.random.normal(k, shape, jnp.float32)

    L = DEPTH
    return {
        "x": jax.random.normal(ks[0], (BATCH, SEQ, D_MODEL), jnp.float32),
        "positions": jnp.broadcast_to(jnp.arange(SEQ, dtype=jnp.int32), (BATCH, SEQ)),
        "norm_mix": gain(ks[1], (L, D_MODEL)),
        "w_in": w(ks[2], (L, D_MODEL, IN_COLS), D_MODEL),
        "b_igate": -2.0 + 0.1 * jax.random.normal(ks[3], (L, M_HEADS), jnp.float32),
        "b_fgate": jax.random.uniform(ks[4], (L, M_HEADS), jnp.float32, 3.0, 6.0),
        "m_head_norm": gain(ks[5], (L, M_V_W)),
        "w_out_m": w(ks[6], (L, M_V_W, D_MODEL), M_V_W),
        "q_a_norm": gain(ks[7], (L, A_Q_LORA)),
        "w_uq": w(ks[8], (L, A_Q_LORA, A_HEADS * A_QK), A_Q_LORA),
        "kv_a_norm": gain(ks[9], (L, A_KV_LORA)),
        "w_ukv": w(ks[10], (L, A_KV_LORA, A_HEADS * (A_NOPE + A_DV)), A_KV_LORA),
        "qk_norm_q": gain(ks[11], (L, A_QK)),
        "qk_norm_k": gain(ks[12], (L, A_QK)),
        "w_out_a": w(ks[13], (L, A_V_W, D_MODEL), A_V_W),
        "w_out": w(ks[14], (L, D_MODEL, D_MODEL), D_MODEL),
        "norm_mlp": gain(ks[15], (L, D_MODEL)),
        "w_up": w(ks[16], (L, D_MODEL, D_FF), D_MODEL),
        "w_down": w(ks[17], (L, D_FF, D_MODEL), D_FF),
    }


def reference(x, positions, norm_mix, w_in, b_igate, b_fgate, m_head_norm, w_out_m,
              q_a_norm, w_uq, kv_a_norm, w_ukv, qk_norm_q, qk_norm_k, w_out_a,
              w_out, norm_mlp, w_up, w_down):
    inv_freq = ROPE_THETA ** (-jnp.arange(0, A_ROPE, 2, dtype=jnp.float32) / A_ROPE)
    ang = positions.astype(jnp.float32)[..., None] * inv_freq
    cos = jnp.cos(ang)[:, :, None, :]
    sin = jnp.sin(ang)[:, :, None, :]
    for l in range(DEPTH):
        x = hybrid_layer(x, cos, sin, norm_mix[l], w_in[l], b_igate[l], b_fgate[l],
                         m_head_norm[l], w_out_m[l], q_a_norm[l], w_uq[l], kv_a_norm[l],
                         w_ukv[l], qk_norm_q[l], qk_norm_k[l], w_out_a[l], w_out[l],
                         norm_mlp[l], w_up[l], w_down[l])
    return x
```

```python
import functools
import math

import jax
import jax.numpy as jnp
from jax import lax
from jax.experimental import pallas as pl
from jax.experimental.pallas import tpu as pltpu

F32 = jnp.float32
BF16 = jnp.bfloat16

M_HEADS = 8
M_DQK = 256
M_DV = 512
GATE_SOFTCAP = 15.0
A_HEADS = 32
A_NOPE = 128
A_ROPE = 64
A_QK = A_NOPE + A_ROPE
A_DV = 128
ROPE_THETA = 10000.0
EPS = 1e-6

V7X_LANES = 128
V7X_SUBLANES = 8
V7X_VMEM_BYTES = 64 * 1024 * 1024
V7X_VMEM_REQUEST_CAP = V7X_VMEM_BYTES - 6 * 1024 * 1024

A_HEAD_PAD = 2 * V7X_LANES
ROPE_HALF = A_ROPE // 2
MLSTM_CHUNK = 256
LOG2E = math.log2(math.e)


def _vmem_limit(block_bytes, scratch_bytes=0, temp_bytes=0):
    need = 2 * sum(block_bytes) + scratch_bytes + temp_bytes + (2 << 20)
    return int(min(max(need, 16 << 20), V7X_VMEM_REQUEST_CAP))


def _nbytes(shape, dtype):
    return math.prod(shape) * jnp.dtype(dtype).itemsize


def _rmsnorm_body(x_ref, g_ref, o_ref):
    x = x_ref[...]
    ms = jnp.mean(x * x, axis=-1, keepdims=True)
    o_ref[...] = (x * lax.rsqrt(ms + EPS) * g_ref[...]).astype(o_ref.dtype)


def _rmsnorm(x, g, *, tm=256):
    m, d = x.shape
    return pl.pallas_call(
        _rmsnorm_body,
        grid=(m // tm,),
        in_specs=[pl.BlockSpec((tm, d), lambda i: (i, 0)),
                  pl.BlockSpec((1, d), lambda i: (0, 0))],
        out_specs=pl.BlockSpec((tm, d), lambda i: (i, 0)),
        out_shape=jax.ShapeDtypeStruct((m, d), BF16),
        compiler_params=pltpu.CompilerParams(
            dimension_semantics=("parallel",),
            vmem_limit_bytes=_vmem_limit([_nbytes((tm, d), F32), _nbytes((tm, d), BF16)],
                                         temp_bytes=2 * _nbytes((tm, d), F32))),
        name="rmsnorm",
    )(x, g.reshape(1, d))


def _ep_identity(acc):
    return acc


def _ep_relu2(acc):
    return jnp.square(jnp.maximum(acc, 0.0))


def _ep_gate(acc, g):
    return jax.nn.sigmoid(g.astype(F32)) * acc


def _ep_gate_add(acc, g, t):
    return t + jax.nn.sigmoid(g.astype(F32)) * acc


def _ep_residual(acc, r):
    return r + acc


def _mm_body(*refs, nk, n_extra, epilogue):
    a_ref, w_ref = refs[0], refs[1]
    extra_refs = refs[2:2 + n_extra]
    o_ref = refs[2 + n_extra]
    part = jnp.dot(a_ref[...], w_ref[...], preferred_element_type=F32)

    def finish(acc):
        o_ref[...] = epilogue(acc, *[r[...] for r in extra_refs]).astype(o_ref.dtype)

    if nk == 1:
        finish(part)
        return
    acc_ref = refs[3 + n_extra]
    k = pl.program_id(2)

    @pl.when(k == 0)
    def _():
        acc_ref[...] = part

    @pl.when(jnp.logical_and(k > 0, k < nk - 1))
    def _():
        acc_ref[...] += part

    @pl.when(k == nk - 1)
    def _():
        finish(acc_ref[...] + part)


def _matmul(a, w, *, bm, bn, bk, out_dtype, epilogue=_ep_identity, extras=(), name):
    m, kdim = a.shape
    n = w.shape[1]
    nk = kdim // bk
    in_specs = [pl.BlockSpec((bm, bk), lambda i, j, k: (i, k)),
                pl.BlockSpec((bk, bn), lambda i, j, k: (k, j))]
    blocks = [_nbytes((bm, bk), a.dtype), _nbytes((bk, bn), w.dtype), _nbytes((bm, bn), out_dtype)]
    for arr, off in extras:
        in_specs.append(pl.BlockSpec((bm, bn), lambda i, j, k, off=off: (i, off + j)))
        blocks.append(_nbytes((bm, bn), arr.dtype))
    scratch = [pltpu.VMEM((bm, bn), F32)] if nk > 1 else []
    return pl.pallas_call(
        functools.partial(_mm_body, nk=nk, n_extra=len(extras), epilogue=epilogue),
        grid=(m // bm, n // bn, nk),
        in_specs=in_specs,
        out_specs=pl.BlockSpec((bm, bn), lambda i, j, k: (i, j)),
        out_shape=jax.ShapeDtypeStruct((m, n), out_dtype),
        scratch_shapes=scratch,
        compiler_params=pltpu.CompilerParams(
            dimension_semantics=("parallel", "parallel", "arbitrary"),
            vmem_limit_bytes=_vmem_limit(blocks, scratch_bytes=(nk > 1) * _nbytes((bm, bn), F32),
                                         temp_bytes=3 * _nbytes((bm, bn), F32))),
        name=name,
    )(a, w, *[arr for arr, _ in extras])


def _rope_body(ang_ref, cmask_ref, ssign_ref, cos_ref, sin_ref):
    ang = ang_ref[...]
    cos_ref[...] = jnp.cos(ang) * cmask_ref[...]
    sin_ref[...] = jnp.sin(ang) * ssign_ref[...]


def _rope_tables(ang128, *, tm=1024):
    m = ang128.shape[0]
    zeros = jnp.zeros((ROPE_HALF,), F32)
    ones = jnp.ones((ROPE_HALF,), F32)
    cmask = jnp.concatenate([ones, zeros, ones, zeros]).reshape(1, V7X_LANES)
    ssign = jnp.concatenate([-ones, zeros, ones, zeros]).reshape(1, V7X_LANES)
    row = pl.BlockSpec((tm, V7X_LANES), lambda i: (i, 0))
    vec = pl.BlockSpec((1, V7X_LANES), lambda i: (0, 0))
    return pl.pallas_call(
        _rope_body,
        grid=(m // tm,),
        in_specs=[row, vec, vec],
        out_specs=[row, row],
        out_shape=[jax.ShapeDtypeStruct((m, V7X_LANES), F32)] * 2,
        compiler_params=pltpu.CompilerParams(dimension_semantics=("parallel",)),
        name="rope_tables",
    )(ang128, cmask, ssign)


def _rotate(x, cos, sin):
    return x * cos + pltpu.roll(x, shift=V7X_LANES // 2, axis=1) * sin


def _split_bf16(x):
    hi = x.astype(BF16)
    r1 = x - hi.astype(F32)
    mid = r1.astype(BF16)
    lo = (r1 - mid.astype(F32)).astype(BF16)
    return hi, mid, lo


def _gate_body(i_ref, f_ref, bi_ref, bf_ref, bcol_ref, gcol_ref, grow_ref, *, chunk):
    li = GATE_SOFTCAP * jnp.tanh((i_ref[...] + bi_ref[...]) / GATE_SOFTCAP)
    z = GATE_SOFTCAP * jnp.tanh((f_ref[...] + bf_ref[...]) / GATE_SOFTCAP)
    lf = jnp.minimum(z, 0.0) - jnp.log(1.0 + jnp.exp(-jnp.abs(z)))
    row = lax.broadcasted_iota(jnp.int32, (chunk, chunk), 0)
    col = lax.broadcasted_iota(jnp.int32, (chunk, chunk), 1)
    tril = (col <= row).astype(BF16)
    b = sum(jnp.dot(tril, piece, preferred_element_type=F32) for piece in _split_bf16(lf))
    g = li - b
    bcol_ref[...] = b
    gcol_ref[...] = g
    grow_ref[...] = g.T[:V7X_SUBLANES, :]


def _mlstm_gates(small, b_i, b_f, *, i_blk, f_blk, chunk):
    m = small.shape[0]
    pad = lambda b: jnp.pad(b.astype(F32), (0, V7X_LANES - b.shape[0])).reshape(1, V7X_LANES)
    col = pl.BlockSpec((chunk, V7X_LANES), lambda c: (c, 0))
    vec = pl.BlockSpec((1, V7X_LANES), lambda c: (0, 0))
    return pl.pallas_call(
        functools.partial(_gate_body, chunk=chunk),
        grid=(m // chunk,),
        in_specs=[pl.BlockSpec((chunk, V7X_LANES), lambda c: (c, i_blk)),
                  pl.BlockSpec((chunk, V7X_LANES), lambda c: (c, f_blk)),
                  vec, vec],
        out_specs=[col, col, pl.BlockSpec((V7X_SUBLANES, chunk), lambda c: (0, c))],
        out_shape=[jax.ShapeDtypeStruct((m, V7X_LANES), F32),
                   jax.ShapeDtypeStruct((m, V7X_LANES), F32),
                   jax.ShapeDtypeStruct((V7X_SUBLANES, m), F32)],
        compiler_params=pltpu.CompilerParams(dimension_semantics=("parallel",)),
        name="mlstm_gates",
    )(small, small, pad(b_i), pad(b_f))


def _mlstm_body(q_ref, k_ref, v_ref, o_ref, bcol_ref, gcol_ref, grow_ref, hn_ref, out_ref,
                c_ref, n_ref, m_ref, *, chunk):
    head = pl.program_id(1)

    @pl.when(pl.program_id(2) == 0)
    def _():
        c_ref[...] = jnp.zeros_like(c_ref)
        n_ref[...] = jnp.zeros_like(n_ref)
        m_ref[...] = jnp.zeros_like(m_ref)

    lane_sel = lax.broadcasted_iota(jnp.int32, (1, V7X_LANES), 1) == head
    bcol = jnp.sum(jnp.where(lane_sel, bcol_ref[...], 0.0), axis=-1, keepdims=True)
    gcol = jnp.sum(jnp.where(lane_sel, gcol_ref[...], 0.0), axis=-1, keepdims=True)
    sub_sel = lax.broadcasted_iota(jnp.int32, (V7X_SUBLANES, 1), 0) == head
    grow = jnp.sum(jnp.where(sub_sel, grow_ref[...], 0.0), axis=0, keepdims=True)

    m_prev = m_ref[...]
    row = lax.broadcasted_iota(jnp.int32, (chunk, chunk), 0)
    col = lax.broadcasted_iota(jnp.int32, (chunk, chunk), 1)
    dmat = jnp.where(col <= row, bcol + grow, -jnp.inf)
    inter = bcol + m_prev
    mj = jnp.maximum(inter, jnp.max(dmat, axis=-1, keepdims=True))
    w_inter = jnp.exp(inter - mj)

    q = q_ref[...]
    k = k_ref[...]
    v = v_ref[...]
    scale = M_DQK ** -0.5
    qk = lax.dot_general(q, k, (((1,), (1,)), ((), ())), preferred_element_type=F32) * scale
    p = jnp.exp(dmat - mj) * qk
    cq = jnp.dot(q, c_ref[...].astype(BF16), preferred_element_type=F32) * scale
    num = w_inter * cq + jnp.dot(p.astype(BF16), v, preferred_element_type=F32)
    qn = jnp.sum(q.astype(F32) * n_ref[...], axis=-1, keepdims=True) * scale
    nq = w_inter * qn + jnp.sum(p, axis=-1, keepdims=True)
    hv = num / jnp.maximum(jnp.abs(nq), jnp.exp(-mj))

    ms = jnp.mean(hv * hv, axis=-1, keepdims=True)
    hn = hv * lax.rsqrt(ms + EPS) * hn_ref[...]
    out_ref[...] = (hn * jax.nn.sigmoid(o_ref[...].astype(F32))).astype(out_ref.dtype)

    b_last = bcol[chunk - 1:chunk, :]
    acol = b_last + gcol
    m_new = jnp.maximum(b_last + m_prev, jnp.max(acol, axis=0, keepdims=True))
    decay = jnp.exp(b_last + m_prev - m_new)
    kw = k.astype(F32) * jnp.exp(acol - m_new)
    c_ref[...] = decay * c_ref[...] + lax.dot_general(
        kw.astype(BF16), v, (((0,), (0,)), ((), ())), preferred_element_type=F32)
    n_ref[...] = decay * n_ref[...] + jnp.sum(kw, axis=0, keepdims=True)
    m_ref[...] = m_new


def _mlstm(proj, bcol, gcol, grow, head_gain, *, batch, seq, chunk):
    m = proj.shape[0]
    nc = seq // chunk
    k_blk0 = (M_HEADS * M_DQK) // M_DQK
    v_blk0 = (2 * M_HEADS * M_DQK) // M_DV
    o_blk0 = v_blk0 + M_HEADS
    rows = lambda b, h, c: b * nc + c
    return pl.pallas_call(
        functools.partial(_mlstm_body, chunk=chunk),
        grid=(batch, M_HEADS, nc),
        in_specs=[
            pl.BlockSpec((chunk, M_DQK), lambda b, h, c: (rows(b, h, c), h)),
            pl.BlockSpec((chunk, M_DQK), lambda b, h, c: (rows(b, h, c), k_blk0 + h)),
            pl.BlockSpec((chunk, M_DV), lambda b, h, c: (rows(b, h, c), v_blk0 + h)),
            pl.BlockSpec((chunk, M_DV), lambda b, h, c: (rows(b, h, c), o_blk0 + h)),
            pl.BlockSpec((chunk, V7X_LANES), lambda b, h, c: (rows(b, h, c), 0)),
            pl.BlockSpec((chunk, V7X_LANES), lambda b, h, c: (rows(b, h, c), 0)),
            pl.BlockSpec((V7X_SUBLANES, chunk), lambda b, h, c: (0, rows(b, h, c))),
            pl.BlockSpec((1, M_DV), lambda b, h, c: (0, h)),
        ],
        out_specs=pl.BlockSpec((chunk, M_DV), lambda b, h, c: (rows(b, h, c), h)),
        out_shape=jax.ShapeDtypeStruct((m, M_HEADS * M_DV), BF16),
        scratch_shapes=[pltpu.VMEM((M_DQK, M_DV), F32),
                        pltpu.VMEM((1, M_DQK), F32),
                        pltpu.VMEM((1, 1), F32)],
        compiler_params=pltpu.CompilerParams(
            dimension_semantics=("parallel", "parallel", "arbitrary")),
        name="mlstm",
    )(proj, proj, proj, proj, bcol, gcol, grow, head_gain.reshape(1, M_HEADS * M_DV))


def _row_rmsnorm(a, gain):
    ms = jnp.mean(a * a, axis=-1, keepdims=True)
    return (a * lax.rsqrt(ms + EPS) * gain).astype(BF16)


def _qproj_body(a_ref, an_ref, w_ref, gq_ref, cos_ref, sin_ref, o_ref, *, heads_per_block):
    a = _row_rmsnorm(a_ref[...], an_ref[...])
    acc = jnp.dot(a, w_ref[...], preferred_element_type=F32)
    cos = cos_ref[...]
    sin = sin_ref[...]
    gq = gq_ref[...]
    for hh in range(heads_per_block):
        blk = acc[:, hh * A_HEAD_PAD:(hh + 1) * A_HEAD_PAD]
        ss = jnp.sum(blk * blk, axis=-1, keepdims=True)
        qn = blk * lax.rsqrt(ss * (1.0 / A_QK) + EPS) * gq
        rope = _rotate(qn[:, V7X_LANES:], cos, sin)
        o_ref[:, hh * A_HEAD_PAD:hh * A_HEAD_PAD + V7X_LANES] = qn[:, :V7X_LANES].astype(o_ref.dtype)
        o_ref[:, hh * A_HEAD_PAD + V7X_LANES:(hh + 1) * A_HEAD_PAD] = rope.astype(o_ref.dtype)


def _qproj(small, a_norm, w_q, gq, cos, sin, *, rank, bm=512, heads_per_block=4):
    m = small.shape[0]
    bn = heads_per_block * A_HEAD_PAD
    n = w_q.shape[1]
    rowvec = pl.BlockSpec((bm, V7X_LANES), lambda i, j: (i, 0))
    return pl.pallas_call(
        functools.partial(_qproj_body, heads_per_block=heads_per_block),
        grid=(m // bm, n // bn),
        in_specs=[pl.BlockSpec((bm, rank), lambda i, j: (i, 0)),
                  pl.BlockSpec((1, rank), lambda i, j: (0, 0)),
                  pl.BlockSpec((rank, bn), lambda i, j: (0, j)),
                  pl.BlockSpec((1, A_HEAD_PAD), lambda i, j: (0, 0)),
                  rowvec, rowvec],
        out_specs=pl.BlockSpec((bm, bn), lambda i, j: (i, j)),
        out_shape=jax.ShapeDtypeStruct((m, n), BF16),
        compiler_params=pltpu.CompilerParams(
            dimension_semantics=("parallel", "parallel"),
            vmem_limit_bytes=_vmem_limit([_nbytes((bm, rank), F32), _nbytes((rank, bn), BF16),
                                          _nbytes((bm, bn), BF16)],
                                         temp_bytes=4 * _nbytes((bm, bn), F32))),
        name="mla_qproj",
    )(small, a_norm.reshape(1, rank), w_q, gq, cos, sin)


def _kvproj_body(a_ref, an_ref, w_ref, kr_ref, gkn_ref, gkr_ref, cos_ref, sin_ref, k_ref, v_ref, *,
                 heads_per_block):
    a = _row_rmsnorm(a_ref[...], an_ref[...])
    acc = jnp.dot(a, w_ref[...], preferred_element_type=F32)
    kr = kr_ref[...]
    ss_rope = jnp.sum(kr * kr, axis=-1, keepdims=True)
    k_rot = _rotate(kr * gkr_ref[...], cos_ref[...], sin_ref[...])
    gkn = gkn_ref[...]
    for hh in range(heads_per_block):
        kn = acc[:, hh * A_NOPE:(hh + 1) * A_NOPE]
        ss = jnp.sum(kn * kn, axis=-1, keepdims=True) + ss_rope
        rs = lax.rsqrt(ss * (1.0 / A_QK) + EPS)
        k_ref[:, hh * A_HEAD_PAD:hh * A_HEAD_PAD + V7X_LANES] = (kn * rs * gkn).astype(k_ref.dtype)
        k_ref[:, hh * A_HEAD_PAD + V7X_LANES:(hh + 1) * A_HEAD_PAD] = (k_rot * rs).astype(k_ref.dtype)
    v_ref[...] = acc[:, heads_per_block * A_NOPE:].astype(v_ref.dtype)


def _kvproj(small, a_norm, w_kv, gkn, gkr, cos, sin, *, a_blk, rank, rope_blk, bm=512, heads_per_block=4):
    m = small.shape[0]
    bn = heads_per_block * (A_NOPE + A_DV)
    n_blocks = w_kv.shape[1] // bn
    rowvec = pl.BlockSpec((bm, V7X_LANES), lambda i, j: (i, 0))
    lanevec = pl.BlockSpec((1, V7X_LANES), lambda i, j: (0, 0))
    return pl.pallas_call(
        functools.partial(_kvproj_body, heads_per_block=heads_per_block),
        grid=(m // bm, n_blocks),
        in_specs=[pl.BlockSpec((bm, rank), lambda i, j: (i, a_blk)),
                  pl.BlockSpec((1, rank), lambda i, j: (0, 0)),
                  pl.BlockSpec((rank, bn), lambda i, j: (0, j)),
                  pl.BlockSpec((bm, V7X_LANES), lambda i, j: (i, rope_blk)),
                  lanevec, lanevec, rowvec, rowvec],
        out_specs=[pl.BlockSpec((bm, heads_per_block * A_HEAD_PAD), lambda i, j: (i, j)),
                   pl.BlockSpec((bm, heads_per_block * A_DV), lambda i, j: (i, j))],
        out_shape=[jax.ShapeDtypeStruct((m, A_HEADS * A_HEAD_PAD), BF16),
                   jax.ShapeDtypeStruct((m, A_HEADS * A_DV), BF16)],
        compiler_params=pltpu.CompilerParams(
            dimension_semantics=("parallel", "parallel"),
            vmem_limit_bytes=_vmem_limit([_nbytes((bm, rank), F32), _nbytes((rank, bn), BF16),
                                          _nbytes((bm, bn), BF16), _nbytes((bm, bn // 2), BF16)],
                                         temp_bytes=4 * _nbytes((bm, bn), F32))),
        name="mla_kvproj",
    )(small, a_norm.reshape(1, rank), w_kv, small, gkn, gkr, cos, sin)


def _flash_body(it_ref, jt_ref, q_ref, k_ref, v_ref, o_ref, m_ref, l_ref, acc_ref, *, tq, tk):
    t = pl.program_id(2)
    i = it_ref[t]
    j = jt_ref[t]

    @pl.when(j == 0)
    def _():
        m_ref[...] = jnp.full_like(m_ref, -jnp.inf)
        l_ref[...] = jnp.zeros_like(l_ref)
        acc_ref[...] = jnp.zeros_like(acc_ref)

    def step(masked):
        s = lax.dot_general(q_ref[...], k_ref[...], (((1,), (1,)), ((), ())), preferred_element_type=F32)
        if masked:
            row = lax.broadcasted_iota(jnp.int32, (tq, tk), 0) + i * tq
            col = lax.broadcasted_iota(jnp.int32, (tq, tk), 1) + j * tk
            s = jnp.where(col <= row, s, -jnp.inf)
        m_prev = m_ref[...]
        m_new = jnp.maximum(m_prev, jnp.max(s, axis=-1, keepdims=True))
        alpha = jnp.exp2(m_prev - m_new)
        p = jnp.exp2(s - m_new)
        l_ref[...] = alpha * l_ref[...] + jnp.sum(p, axis=-1, keepdims=True)
        acc_ref[...] = alpha * acc_ref[...] + jnp.dot(p.astype(BF16), v_ref[...], preferred_element_type=F32)
        m_ref[...] = m_new

    needs_mask = (j + 1) * tk - 1 > i * tq

    @pl.when(jnp.logical_not(needs_mask))
    def _():
        step(False)

    @pl.when(needs_mask)
    def _():
        step(True)

    @pl.when((j + 1) * tk == (i + 1) * tq)
    def _():
        o_ref[...] = (acc_ref[...] / l_ref[...]).astype(o_ref.dtype)


def _flash_attention(q, k, v, *, batch, seq, tq=512, tk=512):
    m = q.shape[0]
    nq = seq // tq
    nkb = seq // tk
    pairs = [(i, j) for i in range(nq) for j in range(((i + 1) * tq) // tk)]
    i_tab = jnp.asarray([p[0] for p in pairs], jnp.int32)
    j_tab = jnp.asarray([p[1] for p in pairs], jnp.int32)
    grid_spec = pltpu.PrefetchScalarGridSpec(
        num_scalar_prefetch=2,
        grid=(batch, A_HEADS, len(pairs)),
        in_specs=[
            pl.BlockSpec((tq, A_HEAD_PAD), lambda b, h, t, it, jt: (b * nq + it[t], h)),
            pl.BlockSpec((tk, A_HEAD_PAD), lambda b, h, t, it, jt: (b * nkb + jt[t], h)),
            pl.BlockSpec((tk, A_DV), lambda b, h, t, it, jt: (b * nkb + jt[t], h)),
        ],
        out_specs=pl.BlockSpec((tq, A_DV), lambda b, h, t, it, jt: (b * nq + it[t], h)),
        scratch_shapes=[pltpu.VMEM((tq, 1), F32), pltpu.VMEM((tq, 1), F32), pltpu.VMEM((tq, A_DV), F32)],
    )
    return pl.pallas_call(
        functools.partial(_flash_body, tq=tq, tk=tk),
        grid_spec=grid_spec,
        out_shape=jax.ShapeDtypeStruct((m, A_HEADS * A_DV), BF16),
        compiler_params=pltpu.CompilerParams(
            dimension_semantics=("parallel", "parallel", "arbitrary"),
            vmem_limit_bytes=_vmem_limit([_nbytes((tq, A_HEAD_PAD), BF16), _nbytes((tk, A_HEAD_PAD), BF16),
                                          _nbytes((tk, A_DV), BF16), _nbytes((tq, A_DV), BF16)],
                                         temp_bytes=6 * _nbytes((tq, tk), F32))),
        name="mla_flash",
    )(i_tab, j_tab, q, k, v)


def _rope_lane_layout(x):
    z = jnp.zeros(x.shape[:-1] + (ROPE_HALF,), x.dtype)
    return jnp.concatenate([x[..., :ROPE_HALF], z, x[..., ROPE_HALF:], z], axis=-1)


def _head_lane_layout(x):
    return jnp.concatenate([x[..., :A_NOPE], _rope_lane_layout(x[..., A_NOPE:])], axis=-1)


def _layer(xf, cos, sin, norm_mix, w_in, b_igate, b_fgate, m_head_norm, w_out_m, q_a_norm, w_uq,
           kv_a_norm, w_ukv, qk_norm_q, qk_norm_k, w_out_a, w_out, norm_mlp, w_up, w_down, *, batch, seq):
    d_model = xf.shape[1]
    q_lora = w_uq.shape[0]
    kv_lora = w_ukv.shape[0]
    m_qk_w = M_HEADS * M_DQK
    m_v_w = M_HEADS * M_DV
    o_i = 2 * m_qk_w + 2 * m_v_w
    o_f = o_i + M_HEADS
    o_cq = o_f + M_HEADS
    o_kva = o_cq + q_lora
    o_ga = o_kva + kv_lora + A_ROPE
    w_main = jnp.concatenate([w_in[:, :o_i], w_in[:, o_ga:]], axis=1).astype(BF16)
    zcols = lambda n: jnp.zeros((d_model, n), w_in.dtype)
    gate_pad = V7X_LANES - M_HEADS
    w_small = jnp.concatenate([
        w_in[:, o_cq:o_kva],
        w_in[:, o_kva:o_kva + kv_lora],
        _rope_lane_layout(w_in[:, o_kva + kv_lora:o_ga]),
        w_in[:, o_i:o_f], zcols(gate_pad),
        w_in[:, o_f:o_cq], zcols(gate_pad),
    ], axis=1).astype(BF16)
    kv_blk = q_lora // kv_lora
    rope_blk = (q_lora + kv_lora) // V7X_LANES
    i_blk = rope_blk + 1
    f_blk = i_blk + 1

    w_q = _head_lane_layout(w_uq.reshape(q_lora, A_HEADS, A_QK)).reshape(q_lora, A_HEADS * A_HEAD_PAD)
    w_q = w_q.astype(BF16)
    gq = (_head_lane_layout(qk_norm_q) * (A_QK ** -0.5 * LOG2E)).reshape(1, A_HEAD_PAD)
    hpb = 4
    w_kv4 = w_ukv.reshape(kv_lora, A_HEADS // hpb, hpb, A_NOPE + A_DV)
    w_kv = jnp.concatenate([w_kv4[..., :A_NOPE].reshape(kv_lora, A_HEADS // hpb, hpb * A_NOPE),
                            w_kv4[..., A_NOPE:].reshape(kv_lora, A_HEADS // hpb, hpb * A_DV)], axis=-1)
    w_kv = w_kv.reshape(kv_lora, A_HEADS * (A_NOPE + A_DV)).astype(BF16)
    gkn = qk_norm_k[:A_NOPE].reshape(1, V7X_LANES)
    gkr = _rope_lane_layout(qk_norm_k[A_NOPE:]).reshape(1, V7X_LANES)

    h = _rmsnorm(xf, norm_mix)
    proj = _matmul(h, w_main, bm=1024, bn=1024, bk=d_model, out_dtype=BF16, name="in_proj_main")
    small = _matmul(h, w_small, bm=1024, bn=w_small.shape[1] // 3, bk=d_model, out_dtype=F32,
                    name="in_proj_small")

    bcol, gcol, grow = _mlstm_gates(small, b_igate, b_fgate, i_blk=i_blk, f_blk=f_blk, chunk=MLSTM_CHUNK)
    hm = _mlstm(proj, bcol, gcol, grow, m_head_norm, batch=batch, seq=seq, chunk=MLSTM_CHUNK)

    qh = _qproj(small, q_a_norm, w_q, gq, cos, sin, rank=q_lora, heads_per_block=hpb)
    kh, vh = _kvproj(small, kv_a_norm, w_kv, gkn, gkr, cos, sin, a_blk=kv_blk, rank=kv_lora,
                     rope_blk=rope_blk, heads_per_block=hpb)
    att = _flash_attention(qh, kh, vh, batch=batch, seq=seq)

    ga_blk0 = o_i // 1024
    gb_blk0 = ga_blk0 + d_model // 1024
    t = _matmul(hm, w_out_m.astype(BF16), bm=1024, bn=1024, bk=m_v_w, out_dtype=F32,
                epilogue=_ep_gate, extras=[(proj, ga_blk0)], name="out_proj_mlstm")
    merged = _matmul(att, w_out_a.astype(BF16), bm=1024, bn=1024, bk=A_HEADS * A_DV, out_dtype=BF16,
                     epilogue=_ep_gate_add, extras=[(proj, gb_blk0), (t, 0)], name="out_proj_mla")
    xf = _matmul(merged, w_out.astype(BF16), bm=1024, bn=1024, bk=d_model, out_dtype=F32,
                 epilogue=_ep_residual, extras=[(xf, 0)], name="out_proj")

    h2 = _rmsnorm(xf, norm_mlp)
    u = _matmul(h2, w_up.astype(BF16), bm=1024, bn=1024, bk=d_model, out_dtype=BF16,
                epilogue=_ep_relu2, name="mlp_up")
    xf = _matmul(u, w_down.astype(BF16), bm=1024, bn=1024, bk=2048, out_dtype=F32,
                 epilogue=_ep_residual, extras=[(xf, 0)], name="mlp_down")
    return xf


def kernel(x, positions, norm_mix, w_in, b_igate, b_fgate, m_head_norm, w_out_m, q_a_norm, w_uq,
           kv_a_norm, w_ukv, qk_norm_q, qk_norm_k, w_out_a, w_out, norm_mlp, w_up, w_down):
    batch, seq, d_model = x.shape
    depth = w_in.shape[0]
    m = batch * seq
    inv_freq = ROPE_THETA ** (-jnp.arange(0, A_ROPE, 2, dtype=F32) / A_ROPE)
    ang = positions.astype(F32).reshape(m, 1) * inv_freq
    zeros = jnp.zeros_like(ang)
    cos, sin = _rope_tables(jnp.concatenate([ang, zeros, ang, zeros], axis=-1))
    xf = x.reshape(m, d_model)
    for l in range(depth):
        xf = _layer(xf, cos, sin, norm_mix[l], w_in[l], b_igate[l], b_fgate[l], m_head_norm[l], w_out_m[l],
                    q_a_norm[l], w_uq[l], kv_a_norm[l], w_ukv[l], qk_norm_q[l], qk_norm_k[l], w_out_a[l],
                    w_out[l], norm_mlp[l], w_up[l], w_down[l], batch=batch, seq=seq)
    return xf.reshape(batch, seq, d_model)
```

```python
import functools
import math

import jax
import jax.numpy as jnp
from jax import lax
from jax.experimental import pallas as pl
from jax.experimental.pallas import tpu as pltpu

F32 = jnp.float32
BF16 = jnp.bfloat16

M_HEADS = 8
M_DQK = 256
M_DV = 512
GATE_SOFTCAP = 15.0
A_HEADS = 32
A_NOPE = 128
A_ROPE = 64
A_QK = A_NOPE + A_ROPE
A_DV = 128
ROPE_THETA = 10000.0
EPS = 1e-6

V7X_LANES = 128
V7X_SUBLANES = 8
V7X_VMEM_BYTES = 64 * 1024 * 1024
V7X_VMEM_REQUEST_CAP = V7X_VMEM_BYTES - 6 * 1024 * 1024

A_HEAD_PAD = 2 * V7X_LANES
ROPE_HALF = A_ROPE // 2
MLSTM_CHUNK = 256
LOG2E = math.log2(math.e)


def _vmem_limit(block_bytes, scratch_bytes=0, temp_bytes=0):
    need = 2 * sum(block_bytes) + scratch_bytes + temp_bytes + (2 << 20)
    return int(min(max(need, 16 << 20), V7X_VMEM_REQUEST_CAP))


def _nbytes(shape, dtype):
    return math.prod(shape) * jnp.dtype(dtype).itemsize


def _rmsnorm_body(x_ref, g_ref, o_ref):
    x = x_ref[...]
    ms = jnp.mean(x * x, axis=-1, keepdims=True)
    o_ref[...] = (x * lax.rsqrt(ms + EPS) * g_ref[...]).astype(o_ref.dtype)


def _rmsnorm(x, g, *, tm=256):
    m, d = x.shape
    return pl.pallas_call(
        _rmsnorm_body,
        grid=(m // tm,),
        in_specs=[pl.BlockSpec((tm, d), lambda i: (i, 0)),
                  pl.BlockSpec((1, d), lambda i: (0, 0))],
        out_specs=pl.BlockSpec((tm, d), lambda i: (i, 0)),
        out_shape=jax.ShapeDtypeStruct((m, d), BF16),
        compiler_params=pltpu.CompilerParams(
            dimension_semantics=("parallel",),
            vmem_limit_bytes=_vmem_limit([_nbytes((tm, d), F32), _nbytes((tm, d), BF16)],
                                         temp_bytes=2 * _nbytes((tm, d), F32))),
        name="rmsnorm",
    )(x, g.reshape(1, d))


def _ep_identity(acc):
    return acc


def _ep_relu2(acc):
    return jnp.square(jnp.maximum(acc, 0.0))


def _ep_gate(acc, g):
    return jax.nn.sigmoid(g.astype(F32)) * acc


def _ep_gate_add(acc, g, t):
    return t + jax.nn.sigmoid(g.astype(F32)) * acc


def _ep_residual(acc, r):
    return r + acc


def _mm_body(*refs, nk, n_extra, epilogue):
    a_ref, w_ref = refs[0], refs[1]
    extra_refs = refs[2:2 + n_extra]
    o_ref = refs[2 + n_extra]
    part = jnp.dot(a_ref[...], w_ref[...], preferred_element_type=F32)

    def finish(acc):
        o_ref[...] = epilogue(acc, *[r[...] for r in extra_refs]).astype(o_ref.dtype)

    if nk == 1:
        finish(part)
        return
    acc_ref = refs[3 + n_extra]
    k = pl.program_id(2)

    @pl.when(k == 0)
    def _():
        acc_ref[...] = part

    @pl.when(jnp.logical_and(k > 0, k < nk - 1))
    def _():
        acc_ref[...] += part

    @pl.when(k == nk - 1)
    def _():
        finish(acc_ref[...] + part)


def _matmul(a, w, *, bm, bn, bk, out_dtype, epilogue=_ep_identity, extras=(), name):
    m, kdim = a.shape
    n = w.shape[1]
    nk = kdim // bk
    in_specs = [pl.BlockSpec((bm, bk), lambda i, j, k: (i, k)),
                pl.BlockSpec((bk, bn), lambda i, j, k: (k, j))]
    blocks = [_nbytes((bm, bk), a.dtype), _nbytes((bk, bn), w.dtype), _nbytes((bm, bn), out_dtype)]
    for arr, off in extras:
        in_specs.append(pl.BlockSpec((bm, bn), lambda i, j, k, off=off: (i, off + j)))
        blocks.append(_nbytes((bm, bn), arr.dtype))
    scratch = [pltpu.VMEM((bm, bn), F32)] if nk > 1 else []
    return pl.pallas_call(
        functools.partial(_mm_body, nk=nk, n_extra=len(extras), epilogue=epilogue),
        grid=(m // bm, n // bn, nk),
        in_specs=in_specs,
        out_specs=pl.BlockSpec((bm, bn), lambda i, j, k: (i, j)),
        out_shape=jax.ShapeDtypeStruct((m, n), out_dtype),
        scratch_shapes=scratch,
        compiler_params=pltpu.CompilerParams(
            dimension_semantics=("parallel", "parallel", "arbitrary"),
            vmem_limit_bytes=_vmem_limit(blocks, scratch_bytes=(nk > 1) * _nbytes((bm, bn), F32),
                                         temp_bytes=3 * _nbytes((bm, bn), F32))),
        name=name,
    )(a, w, *[arr for arr, _ in extras])


def _mm_wres_body(*refs, n_extra, epilogue, k_chunk):
    a_ref, w_ref = refs[0], refs[1]
    extra_refs = refs[2:2 + n_extra]
    o_ref = refs[2 + n_extra]
    wb_ref = refs[3 + n_extra]

    @pl.when(pl.program_id(1) == 0)
    def _():
        for c in range(wb_ref.shape[0] // k_chunk):
            rows = slice(c * k_chunk, (c + 1) * k_chunk)
            wb_ref[rows, :] = w_ref[rows, :].astype(BF16)

    acc = jnp.dot(a_ref[...], wb_ref[...], preferred_element_type=F32)
    o_ref[...] = epilogue(acc, *[r[...] for r in extra_refs]).astype(o_ref.dtype)


def _matmul_wres(a, w_stack, layer, *, n_cols, bm, bn, out_dtype, epilogue=_ep_identity, extras=(), name):
    m, kdim = a.shape
    in_specs = [pl.BlockSpec((bm, kdim), lambda j, i: (i, 0)),
                pl.BlockSpec((None, kdim, bn), lambda j, i: (layer, 0, j))]
    blocks = [_nbytes((bm, kdim), a.dtype), _nbytes((kdim, bn), F32), _nbytes((bm, bn), out_dtype)]
    for arr, off in extras:
        in_specs.append(pl.BlockSpec((bm, bn), lambda j, i, off=off: (i, off + j)))
        blocks.append(_nbytes((bm, bn), arr.dtype))
    return pl.pallas_call(
        functools.partial(_mm_wres_body, n_extra=len(extras), epilogue=epilogue, k_chunk=512),
        grid=(n_cols // bn, m // bm),
        in_specs=in_specs,
        out_specs=pl.BlockSpec((bm, bn), lambda j, i: (i, j)),
        out_shape=jax.ShapeDtypeStruct((m, n_cols), out_dtype),
        scratch_shapes=[pltpu.VMEM((kdim, bn), BF16)],
        compiler_params=pltpu.CompilerParams(
            dimension_semantics=("parallel", "arbitrary"),
            vmem_limit_bytes=_vmem_limit(blocks, scratch_bytes=_nbytes((kdim, bn), BF16),
                                         temp_bytes=3 * _nbytes((bm, bn), F32))),
        name=name,
    )(a, w_stack, *[arr for arr, _ in extras])


def _rope_body(ang_ref, cmask_ref, ssign_ref, cos_ref, sin_ref):
    ang = ang_ref[...]
    cos_ref[...] = jnp.cos(ang) * cmask_ref[...]
    sin_ref[...] = jnp.sin(ang) * ssign_ref[...]


def _rope_tables(ang128, *, tm=1024):
    m = ang128.shape[0]
    zeros = jnp.zeros((ROPE_HALF,), F32)
    ones = jnp.ones((ROPE_HALF,), F32)
    cmask = jnp.concatenate([ones, zeros, ones, zeros]).reshape(1, V7X_LANES)
    ssign = jnp.concatenate([-ones, zeros, ones, zeros]).reshape(1, V7X_LANES)
    row = pl.BlockSpec((tm, V7X_LANES), lambda i: (i, 0))
    vec = pl.BlockSpec((1, V7X_LANES), lambda i: (0, 0))
    return pl.pallas_call(
        _rope_body,
        grid=(m // tm,),
        in_specs=[row, vec, vec],
        out_specs=[row, row],
        out_shape=[jax.ShapeDtypeStruct((m, V7X_LANES), F32)] * 2,
        compiler_params=pltpu.CompilerParams(dimension_semantics=("parallel",)),
        name="rope_tables",
    )(ang128, cmask, ssign)


def _rotate(x, cos, sin):
    return x * cos + pltpu.roll(x, shift=V7X_LANES // 2, axis=1) * sin


def _split_bf16(x):
    hi = x.astype(BF16)
    r1 = x - hi.astype(F32)
    mid = r1.astype(BF16)
    lo = (r1 - mid.astype(F32)).astype(BF16)
    return hi, mid, lo


def _gate_body(i_ref, f_ref, bi_ref, bf_ref, bcol_ref, gcol_ref, grow_ref, *, chunk):
    li = GATE_SOFTCAP * jnp.tanh((i_ref[...] + bi_ref[...]) / GATE_SOFTCAP)
    z = GATE_SOFTCAP * jnp.tanh((f_ref[...] + bf_ref[...]) / GATE_SOFTCAP)
    lf = jnp.minimum(z, 0.0) - jnp.log(1.0 + jnp.exp(-jnp.abs(z)))
    row = lax.broadcasted_iota(jnp.int32, (chunk, chunk), 0)
    col = lax.broadcasted_iota(jnp.int32, (chunk, chunk), 1)
    tril = (col <= row).astype(BF16)
    b = sum(jnp.dot(tril, piece, preferred_element_type=F32) for piece in _split_bf16(lf))
    g = li - b
    bcol_ref[...] = b
    gcol_ref[...] = g
    grow_ref[...] = g.T[:V7X_SUBLANES, :]


def _mlstm_gates(small, b_i, b_f, *, i_blk, f_blk, chunk):
    m = small.shape[0]
    pad = lambda b: jnp.pad(b.astype(F32), (0, V7X_LANES - b.shape[0])).reshape(1, V7X_LANES)
    col = pl.BlockSpec((chunk, V7X_LANES), lambda c: (c, 0))
    vec = pl.BlockSpec((1, V7X_LANES), lambda c: (0, 0))
    return pl.pallas_call(
        functools.partial(_gate_body, chunk=chunk),
        grid=(m // chunk,),
        in_specs=[pl.BlockSpec((chunk, V7X_LANES), lambda c: (c, i_blk)),
                  pl.BlockSpec((chunk, V7X_LANES), lambda c: (c, f_blk)),
                  vec, vec],
        out_specs=[col, col, pl.BlockSpec((V7X_SUBLANES, chunk), lambda c: (0, c))],
        out_shape=[jax.ShapeDtypeStruct((m, V7X_LANES), F32),
                   jax.ShapeDtypeStruct((m, V7X_LANES), F32),
                   jax.ShapeDtypeStruct((V7X_SUBLANES, m), F32)],
        compiler_params=pltpu.CompilerParams(dimension_semantics=("parallel",)),
        name="mlstm_gates",
    )(small, small, pad(b_i), pad(b_f))


def _mlstm_body(q_ref, k_ref, v_ref, o_ref, bcol_ref, gcol_ref, grow_ref, hn_ref, out_ref,
                c_ref, n_ref, m_ref, *, chunk):
    head = pl.program_id(1)

    @pl.when(pl.program_id(2) == 0)
    def _():
        c_ref[...] = jnp.zeros_like(c_ref)
        n_ref[...] = jnp.zeros_like(n_ref)
        m_ref[...] = jnp.zeros_like(m_ref)

    lane_sel = lax.broadcasted_iota(jnp.int32, (1, V7X_LANES), 1) == head
    bcol = jnp.sum(jnp.where(lane_sel, bcol_ref[...], 0.0), axis=-1, keepdims=True)
    gcol = jnp.sum(jnp.where(lane_sel, gcol_ref[...], 0.0), axis=-1, keepdims=True)
    sub_sel = lax.broadcasted_iota(jnp.int32, (V7X_SUBLANES, 1), 0) == head
    grow = jnp.sum(jnp.where(sub_sel, grow_ref[...], 0.0), axis=0, keepdims=True)

    m_prev = m_ref[...]
    row = lax.broadcasted_iota(jnp.int32, (chunk, chunk), 0)
    col = lax.broadcasted_iota(jnp.int32, (chunk, chunk), 1)
    dmat = jnp.where(col <= row, bcol + grow, -jnp.inf)
    inter = bcol + m_prev
    mj = jnp.maximum(inter, jnp.max(dmat, axis=-1, keepdims=True))
    w_inter = jnp.exp(inter - mj)

    q = q_ref[...]
    k = k_ref[...]
    v = v_ref[...]
    scale = M_DQK ** -0.5
    qk = lax.dot_general(q, k, (((1,), (1,)), ((), ())), preferred_element_type=F32) * scale
    p = jnp.exp(dmat - mj) * qk
    cq = jnp.dot(q, c_ref[...].astype(BF16), preferred_element_type=F32) * scale
    num = w_inter * cq + jnp.dot(p.astype(BF16), v, preferred_element_type=F32)
    qn = jnp.sum(q.astype(F32) * n_ref[...], axis=-1, keepdims=True) * scale
    nq = w_inter * qn + jnp.sum(p, axis=-1, keepdims=True)
    hv = num / jnp.maximum(jnp.abs(nq), jnp.exp(-mj))

    ms = jnp.mean(hv * hv, axis=-1, keepdims=True)
    hn = hv * lax.rsqrt(ms + EPS) * hn_ref[...]
    out_ref[...] = (hn * jax.nn.sigmoid(o_ref[...].astype(F32))).astype(out_ref.dtype)

    b_last = bcol[chunk - 1:chunk, :]
    acol = b_last + gcol
    m_new = jnp.maximum(b_last + m_prev, jnp.max(acol, axis=0, keepdims=True))
    decay = jnp.exp(b_last + m_prev - m_new)
    kw = k.astype(F32) * jnp.exp(acol - m_new)
    c_ref[...] = decay * c_ref[...] + lax.dot_general(
        kw.astype(BF16), v, (((0,), (0,)), ((), ())), preferred_element_type=F32)
    n_ref[...] = decay * n_ref[...] + jnp.sum(kw, axis=0, keepdims=True)
    m_ref[...] = m_new


def _mlstm(proj, bcol, gcol, grow, head_gain, *, batch, seq, chunk):
    m = proj.shape[0]
    nc = seq // chunk
    k_blk0 = (M_HEADS * M_DQK) // M_DQK
    v_blk0 = (2 * M_HEADS * M_DQK) // M_DV
    o_blk0 = v_blk0 + M_HEADS
    rows = lambda b, h, c: b * nc + c
    return pl.pallas_call(
        functools.partial(_mlstm_body, chunk=chunk),
        grid=(batch, M_HEADS, nc),
        in_specs=[
            pl.BlockSpec((chunk, M_DQK), lambda b, h, c: (rows(b, h, c), h)),
            pl.BlockSpec((chunk, M_DQK), lambda b, h, c: (rows(b, h, c), k_blk0 + h)),
            pl.BlockSpec((chunk, M_DV), lambda b, h, c: (rows(b, h, c), v_blk0 + h)),
            pl.BlockSpec((chunk, M_DV), lambda b, h, c: (rows(b, h, c), o_blk0 + h)),
            pl.BlockSpec((chunk, V7X_LANES), lambda b, h, c: (rows(b, h, c), 0)),
            pl.BlockSpec((chunk, V7X_LANES), lambda b, h, c: (rows(b, h, c), 0)),
            pl.BlockSpec((V7X_SUBLANES, chunk), lambda b, h, c: (0, rows(b, h, c))),
            pl.BlockSpec((1, M_DV), lambda b, h, c: (0, h)),
        ],
        out_specs=pl.BlockSpec((chunk, M_DV), lambda b, h, c: (rows(b, h, c), h)),
        out_shape=jax.ShapeDtypeStruct((m, M_HEADS * M_DV), BF16),
        scratch_shapes=[pltpu.VMEM((M_DQK, M_DV), F32),
                        pltpu.VMEM((1, M_DQK), F32),
                        pltpu.VMEM((1, 1), F32)],
        compiler_params=pltpu.CompilerParams(
            dimension_semantics=("parallel", "parallel", "arbitrary")),
        name="mlstm",
    )(proj, proj, proj, proj, bcol, gcol, grow, head_gain.reshape(1, M_HEADS * M_DV))


def _row_rmsnorm(a, gain):
    ms = jnp.mean(a * a, axis=-1, keepdims=True)
    return (a * lax.rsqrt(ms + EPS) * gain).astype(BF16)


def _qproj_body(a_ref, an_ref, w_ref, gq_ref, cos_ref, sin_ref, o_ref, *, heads_per_block):
    a = _row_rmsnorm(a_ref[...], an_ref[...])
    acc = jnp.dot(a, w_ref[...], preferred_element_type=F32)
    cos = cos_ref[...]
    sin = sin_ref[...]
    gq = gq_ref[...]
    for hh in range(heads_per_block):
        blk = acc[:, hh * A_HEAD_PAD:(hh + 1) * A_HEAD_PAD]
        ss = jnp.sum(blk * blk, axis=-1, keepdims=True)
        qn = blk * lax.rsqrt(ss * (1.0 / A_QK) + EPS) * gq
        rope = _rotate(qn[:, V7X_LANES:], cos, sin)
        o_ref[:, hh * A_HEAD_PAD:hh * A_HEAD_PAD + V7X_LANES] = qn[:, :V7X_LANES].astype(o_ref.dtype)
        o_ref[:, hh * A_HEAD_PAD + V7X_LANES:(hh + 1) * A_HEAD_PAD] = rope.astype(o_ref.dtype)


def _qproj(small, a_norm, w_q, gq, cos, sin, *, rank, bm=512, heads_per_block=4):
    m = small.shape[0]
    bn = heads_per_block * A_HEAD_PAD
    n = w_q.shape[1]
    rowvec = pl.BlockSpec((bm, V7X_LANES), lambda i, j: (i, 0))
    return pl.pallas_call(
        functools.partial(_qproj_body, heads_per_block=heads_per_block),
        grid=(m // bm, n // bn),
        in_specs=[pl.BlockSpec((bm, rank), lambda i, j: (i, 0)),
                  pl.BlockSpec((1, rank), lambda i, j: (0, 0)),
                  pl.BlockSpec((rank, bn), lambda i, j: (0, j)),
                  pl.BlockSpec((1, A_HEAD_PAD), lambda i, j: (0, 0)),
                  rowvec, rowvec],
        out_specs=pl.BlockSpec((bm, bn), lambda i, j: (i, j)),
        out_shape=jax.ShapeDtypeStruct((m, n), BF16),
        compiler_params=pltpu.CompilerParams(
            dimension_semantics=("parallel", "parallel"),
            vmem_limit_bytes=_vmem_limit([_nbytes((bm, rank), F32), _nbytes((rank, bn), BF16),
                                          _nbytes((bm, bn), BF16)],
                                         temp_bytes=4 * _nbytes((bm, bn), F32))),
        name="mla_qproj",
    )(small, a_norm.reshape(1, rank), w_q, gq, cos, sin)


def _kvproj_body(a_ref, an_ref, w_ref, kr_ref, gkn_ref, gkr_ref, cos_ref, sin_ref, k_ref, v_ref, *,
                 heads_per_block):
    a = _row_rmsnorm(a_ref[...], an_ref[...])
    acc = jnp.dot(a, w_ref[...], preferred_element_type=F32)
    kr = kr_ref[...]
    ss_rope = jnp.sum(kr * kr, axis=-1, keepdims=True)
    k_rot = _rotate(kr * gkr_ref[...], cos_ref[...], sin_ref[...])
    gkn = gkn_ref[...]
    for hh in range(heads_per_block):
        kn = acc[:, hh * A_NOPE:(hh + 1) * A_NOPE]
        ss = jnp.sum(kn * kn, axis=-1, keepdims=True) + ss_rope
        rs = lax.rsqrt(ss * (1.0 / A_QK) + EPS)
        k_ref[:, hh * A_HEAD_PAD:hh * A_HEAD_PAD + V7X_LANES] = (kn * rs * gkn).astype(k_ref.dtype)
        k_ref[:, hh * A_HEAD_PAD + V7X_LANES:(hh + 1) * A_HEAD_PAD] = (k_rot * rs).astype(k_ref.dtype)
    v_ref[...] = acc[:, heads_per_block * A_NOPE:].astype(v_ref.dtype)


def _kvproj(small, a_norm, w_kv, gkn, gkr, cos, sin, *, a_blk, rank, rope_blk, bm=512, heads_per_block=4):
    m = small.shape[0]
    bn = heads_per_block * (A_NOPE + A_DV)
    n_blocks = w_kv.shape[1] // bn
    rowvec = pl.BlockSpec((bm, V7X_LANES), lambda i, j: (i, 0))
    lanevec = pl.BlockSpec((1, V7X_LANES), lambda i, j: (0, 0))
    return pl.pallas_call(
        functools.partial(_kvproj_body, heads_per_block=heads_per_block),
        grid=(m // bm, n_blocks),
        in_specs=[pl.BlockSpec((bm, rank), lambda i, j: (i, a_blk)),
                  pl.BlockSpec((1, rank), lambda i, j: (0, 0)),
                  pl.BlockSpec((rank, bn), lambda i, j: (0, j)),
                  pl.BlockSpec((bm, V7X_LANES), lambda i, j: (i, rope_blk)),
                  lanevec, lanevec, rowvec, rowvec],
        out_specs=[pl.BlockSpec((bm, heads_per_block * A_HEAD_PAD), lambda i, j: (i, j)),
                   pl.BlockSpec((bm, heads_per_block * A_DV), lambda i, j: (i, j))],
        out_shape=[jax.ShapeDtypeStruct((m, A_HEADS * A_HEAD_PAD), BF16),
                   jax.ShapeDtypeStruct((m, A_HEADS * A_DV), BF16)],
        compiler_params=pltpu.CompilerParams(
            dimension_semantics=("parallel", "parallel"),
            vmem_limit_bytes=_vmem_limit([_nbytes((bm, rank), F32), _nbytes((rank, bn), BF16),
                                          _nbytes((bm, bn), BF16), _nbytes((bm, bn // 2), BF16)],
                                         temp_bytes=4 * _nbytes((bm, bn), F32))),
        name="mla_kvproj",
    )(small, a_norm.reshape(1, rank), w_kv, small, gkn, gkr, cos, sin)


FLASH_HEADS = 2


def _attend(q, k, v, m_prev, l_prev, acc_prev, mask):
    s = lax.dot_general(q, k, (((1,), (1,)), ((), ())), preferred_element_type=F32)
    if mask is not None:
        s = jnp.where(mask, s, -jnp.inf)
    m_new = jnp.maximum(m_prev, jnp.max(s, axis=-1, keepdims=True))
    alpha = jnp.exp2(m_prev - m_new)
    p = jnp.exp2(s - m_new)
    l_new = alpha * l_prev + jnp.sum(p, axis=-1, keepdims=True)
    acc_new = alpha * acc_prev + jnp.dot(p.astype(BF16), v, preferred_element_type=F32)
    return m_new, l_new, acc_new


def _flash_body(q_ref, k_ref, v_ref, o_ref, m_ref, l_ref, acc_ref, *, tq):
    i = pl.program_id(2)
    m_ref[...] = jnp.full_like(m_ref, -jnp.inf)
    l_ref[...] = jnp.zeros_like(l_ref)
    acc_ref[...] = jnp.zeros_like(acc_ref)
    row = lax.broadcasted_iota(jnp.int32, (tq, tq), 0)
    col = lax.broadcasted_iota(jnp.int32, (tq, tq), 1)

    def k_block(j, carry):
        rows = pl.ds(pl.multiple_of(j * tq, tq), tq)
        mask = col <= row + (i - j) * tq
        for e in range(FLASH_HEADS):
            m_ref[e], l_ref[e], acc_ref[e] = _attend(
                q_ref[:, e * A_HEAD_PAD:(e + 1) * A_HEAD_PAD], k_ref[rows, e * A_HEAD_PAD:(e + 1) * A_HEAD_PAD],
                v_ref[rows, e * A_DV:(e + 1) * A_DV], m_ref[e], l_ref[e], acc_ref[e], mask)
        return carry

    lax.fori_loop(0, i + 1, k_block, 0)
    for e in range(FLASH_HEADS):
        o_ref[:, e * A_DV:(e + 1) * A_DV] = (acc_ref[e] / l_ref[e]).astype(o_ref.dtype)


def _flash_attention(q, k, v, *, batch, seq, tq=1024):
    m = q.shape[0]
    nq = seq // tq
    hb = FLASH_HEADS
    blocks = [_nbytes((tq, hb * A_HEAD_PAD), BF16), _nbytes((seq, hb * A_HEAD_PAD), BF16),
              _nbytes((seq, hb * A_DV), BF16), _nbytes((tq, hb * A_DV), BF16)]
    scratch = 2 * _nbytes((hb, tq, V7X_LANES), F32) + _nbytes((hb, tq, A_DV), F32)
    return pl.pallas_call(
        functools.partial(_flash_body, tq=tq),
        grid=(batch, A_HEADS // hb, nq),
        in_specs=[
            pl.BlockSpec((tq, hb * A_HEAD_PAD), lambda b, h, i: (b * nq + i, h)),
            pl.BlockSpec((seq, hb * A_HEAD_PAD), lambda b, h, i: (b, h)),
            pl.BlockSpec((seq, hb * A_DV), lambda b, h, i: (b, h)),
        ],
        out_specs=pl.BlockSpec((tq, hb * A_DV), lambda b, h, i: (b * nq + i, h)),
        out_shape=jax.ShapeDtypeStruct((m, A_HEADS * A_DV), BF16),
        scratch_shapes=[pltpu.VMEM((hb, tq, 1), F32), pltpu.VMEM((hb, tq, 1), F32),
                        pltpu.VMEM((hb, tq, A_DV), F32)],
        compiler_params=pltpu.CompilerParams(
            dimension_semantics=("parallel", "parallel", "parallel"),
            vmem_limit_bytes=_vmem_limit(blocks, scratch_bytes=scratch,
                                         temp_bytes=hb * 5 * _nbytes((tq, tq), F32))),
        name="mla_flash",
    )(q, k, v)


def _rope_lane_layout(x):
    z = jnp.zeros(x.shape[:-1] + (ROPE_HALF,), x.dtype)
    return jnp.concatenate([x[..., :ROPE_HALF], z, x[..., ROPE_HALF:], z], axis=-1)


def _head_lane_layout(x):
    return jnp.concatenate([x[..., :A_NOPE], _rope_lane_layout(x[..., A_NOPE:])], axis=-1)


def _layer(xf, cos, sin, layer, norm_mix, w_in_stack, b_igate, b_fgate, m_head_norm, w_out_m_stack, q_a_norm,
           w_uq, kv_a_norm, w_ukv, qk_norm_q, qk_norm_k, w_out_a_stack, w_out_stack, norm_mlp, w_up_stack,
           w_down, *, batch, seq):
    d_model = xf.shape[1]
    q_lora = w_uq.shape[0]
    kv_lora = w_ukv.shape[0]
    m_qk_w = M_HEADS * M_DQK
    m_v_w = M_HEADS * M_DV
    w_in = w_in_stack[layer]
    o_i = 2 * m_qk_w + 2 * m_v_w
    o_f = o_i + M_HEADS
    o_cq = o_f + M_HEADS
    o_kva = o_cq + q_lora
    o_ga = o_kva + kv_lora + A_ROPE
    w_gates = w_in[:, o_ga:].astype(BF16)
    zcols = lambda n: jnp.zeros((d_model, n), w_in.dtype)
    gate_pad = V7X_LANES - M_HEADS
    w_small = jnp.concatenate([
        w_in[:, o_cq:o_kva],
        w_in[:, o_kva:o_kva + kv_lora],
        _rope_lane_layout(w_in[:, o_kva + kv_lora:o_ga]),
        w_in[:, o_i:o_f], zcols(gate_pad),
        w_in[:, o_f:o_cq], zcols(gate_pad),
    ], axis=1).astype(BF16)
    kv_blk = q_lora // kv_lora
    rope_blk = (q_lora + kv_lora) // V7X_LANES
    i_blk = rope_blk + 1
    f_blk = i_blk + 1

    w_q = _head_lane_layout(w_uq.reshape(q_lora, A_HEADS, A_QK)).reshape(q_lora, A_HEADS * A_HEAD_PAD)
    w_q = w_q.astype(BF16)
    gq = (_head_lane_layout(qk_norm_q) * (A_QK ** -0.5 * LOG2E)).reshape(1, A_HEAD_PAD)
    hpb = 4
    w_kv4 = w_ukv.reshape(kv_lora, A_HEADS // hpb, hpb, A_NOPE + A_DV)
    w_kv = jnp.concatenate([w_kv4[..., :A_NOPE].reshape(kv_lora, A_HEADS // hpb, hpb * A_NOPE),
                            w_kv4[..., A_NOPE:].reshape(kv_lora, A_HEADS // hpb, hpb * A_DV)], axis=-1)
    w_kv = w_kv.reshape(kv_lora, A_HEADS * (A_NOPE + A_DV)).astype(BF16)
    gkn = qk_norm_k[:A_NOPE].reshape(1, V7X_LANES)
    gkr = _rope_lane_layout(qk_norm_k[A_NOPE:]).reshape(1, V7X_LANES)

    h = _rmsnorm(xf, norm_mix)
    proj = _matmul_wres(h, w_in_stack, layer, n_cols=o_i, bm=1024, bn=512, out_dtype=BF16,
                        name="in_proj_main")
    gab = _matmul(h, w_gates, bm=1024, bn=1024, bk=d_model, out_dtype=BF16, name="in_proj_gates")
    small = _matmul(h, w_small, bm=1024, bn=w_small.shape[1] // 3, bk=d_model, out_dtype=F32,
                    name="in_proj_small")

    bcol, gcol, grow = _mlstm_gates(small, b_igate, b_fgate, i_blk=i_blk, f_blk=f_blk, chunk=MLSTM_CHUNK)
    hm = _mlstm(proj, bcol, gcol, grow, m_head_norm, batch=batch, seq=seq, chunk=MLSTM_CHUNK)

    qh = _qproj(small, q_a_norm, w_q, gq, cos, sin, rank=q_lora, heads_per_block=hpb)
    kh, vh = _kvproj(small, kv_a_norm, w_kv, gkn, gkr, cos, sin, a_blk=kv_blk, rank=kv_lora,
                     rope_blk=rope_blk, heads_per_block=hpb)
    att = _flash_attention(qh, kh, vh, batch=batch, seq=seq)

    bn = 512
    t = _matmul_wres(hm, w_out_m_stack, layer, n_cols=d_model, bm=1024, bn=bn, out_dtype=F32,
                     epilogue=_ep_gate, extras=[(gab, 0)], name="out_proj_mlstm")
    merged = _matmul_wres(att, w_out_a_stack, layer, n_cols=d_model, bm=1024, bn=bn, out_dtype=BF16,
                          epilogue=_ep_gate_add, extras=[(gab, d_model // bn), (t, 0)], name="out_proj_mla")
    xf = _matmul_wres(merged, w_out_stack, layer, n_cols=d_model, bm=1024, bn=bn, out_dtype=F32,
                      epilogue=_ep_residual, extras=[(xf, 0)], name="out_proj")

    h2 = _rmsnorm(xf, norm_mlp)
    u = _matmul_wres(h2, w_up_stack, layer, n_cols=w_up_stack.shape[2], bm=1024, bn=bn, out_dtype=BF16,
                     epilogue=_ep_relu2, name="mlp_up")
    xf = _matmul(u, w_down.astype(BF16), bm=1024, bn=1024, bk=2048, out_dtype=F32,
                 epilogue=_ep_residual, extras=[(xf, 0)], name="mlp_down")
    return xf


def kernel(x, positions, norm_mix, w_in, b_igate, b_fgate, m_head_norm, w_out_m, q_a_norm, w_uq,
           kv_a_norm, w_ukv, qk_norm_q, qk_norm_k, w_out_a, w_out, norm_mlp, w_up, w_down):
    batch, seq, d_model = x.shape
    depth = w_in.shape[0]
    m = batch * seq
    inv_freq = ROPE_THETA ** (-jnp.arange(0, A_ROPE, 2, dtype=F32) / A_ROPE)
    ang = positions.astype(F32).reshape(m, 1) * inv_freq
    zeros = jnp.zeros_like(ang)
    cos, sin = _rope_tables(jnp.concatenate([ang, zeros, ang, zeros], axis=-1))
    xf = x.reshape(m, d_model)
    for l in range(depth):
        xf = _layer(xf, cos, sin, l, norm_mix[l], w_in, b_igate[l], b_fgate[l], m_head_norm[l], w_out_m,
                    q_a_norm[l], w_uq[l], kv_a_norm[l], w_ukv[l], qk_norm_q[l], qk_norm_k[l], w_out_a,
                    w_out, norm_mlp[l], w_up, w_down[l], batch=batch, seq=seq)
    return xf.reshape(batch, seq, d_model)
```

```python
import functools
import math

import jax
import jax.numpy as jnp
from jax import lax
from jax.experimental import pallas as pl
from jax.experimental.pallas import tpu as pltpu

F32 = jnp.float32
BF16 = jnp.bfloat16

M_HEADS = 8
M_DQK = 256
M_DV = 512
GATE_SOFTCAP = 15.0
A_HEADS = 32
A_NOPE = 128
A_ROPE = 64
A_QK = A_NOPE + A_ROPE
A_DV = 128
ROPE_THETA = 10000.0
EPS = 1e-6

V7X_LANES = 128
V7X_SUBLANES = 8
V7X_VMEM_BYTES = 64 * 1024 * 1024
V7X_VMEM_REQUEST_CAP = V7X_VMEM_BYTES - 6 * 1024 * 1024

A_HEAD_PAD = 2 * V7X_LANES
A_V_PAD = 2 * A_DV
ROPE_HALF = A_ROPE // 2
MLSTM_CHUNK = 256
LOG2E = math.log2(math.e)


def _vmem_limit(block_bytes, scratch_bytes=0, temp_bytes=0):
    need = 2 * sum(block_bytes) + scratch_bytes + temp_bytes + (2 << 20)
    return int(min(max(need, 16 << 20), V7X_VMEM_REQUEST_CAP))


def _nbytes(shape, dtype):
    return math.prod(shape) * jnp.dtype(dtype).itemsize


def _rmsnorm_body(x_ref, g_ref, o_ref):
    x = x_ref[...]
    ms = jnp.mean(x * x, axis=-1, keepdims=True)
    o_ref[...] = (x * lax.rsqrt(ms + EPS) * g_ref[...]).astype(o_ref.dtype)


def _rmsnorm(x, g, *, tm=256):
    m, d = x.shape
    return pl.pallas_call(
        _rmsnorm_body,
        grid=(m // tm,),
        in_specs=[pl.BlockSpec((tm, d), lambda i: (i, 0)),
                  pl.BlockSpec((1, d), lambda i: (0, 0))],
        out_specs=pl.BlockSpec((tm, d), lambda i: (i, 0)),
        out_shape=jax.ShapeDtypeStruct((m, d), BF16),
        compiler_params=pltpu.CompilerParams(
            dimension_semantics=("parallel",),
            vmem_limit_bytes=_vmem_limit([_nbytes((tm, d), F32), _nbytes((tm, d), BF16)],
                                         temp_bytes=2 * _nbytes((tm, d), F32))),
        name="rmsnorm",
    )(x, g.reshape(1, d))


def _ep_identity(acc):
    return acc


def _ep_relu2(acc):
    return jnp.square(jnp.maximum(acc, 0.0))


def _ep_gate(acc, g):
    return jax.nn.sigmoid(g.astype(F32)) * acc


def _ep_gate_add(acc, g, t):
    return t + jax.nn.sigmoid(g.astype(F32)) * acc


def _ep_residual(acc, r):
    return r + acc


def _mm_body(*refs, n_extra, epilogue):
    a_ref, w_ref = refs[0], refs[1]
    extra_refs = refs[2:2 + n_extra]
    o_ref = refs[2 + n_extra]
    acc = jnp.dot(a_ref[...], w_ref[...], preferred_element_type=F32)
    o_ref[...] = epilogue(acc, *[r[...] for r in extra_refs]).astype(o_ref.dtype)


def _matmul(a, w, *, bm, bn, out_dtype, epilogue=_ep_identity, extras=(), name):
    m, kdim = a.shape
    n = w.shape[1]
    in_specs = [pl.BlockSpec((bm, kdim), lambda i, j: (i, 0)),
                pl.BlockSpec((kdim, bn), lambda i, j: (0, j))]
    blocks = [_nbytes((bm, kdim), a.dtype), _nbytes((kdim, bn), w.dtype), _nbytes((bm, bn), out_dtype)]
    for arr, off in extras:
        in_specs.append(pl.BlockSpec((bm, bn), lambda i, j, off=off: (i, off + j)))
        blocks.append(_nbytes((bm, bn), arr.dtype))
    return pl.pallas_call(
        functools.partial(_mm_body, n_extra=len(extras), epilogue=epilogue),
        grid=(m // bm, n // bn),
        in_specs=in_specs,
        out_specs=pl.BlockSpec((bm, bn), lambda i, j: (i, j)),
        out_shape=jax.ShapeDtypeStruct((m, n), out_dtype),
        compiler_params=pltpu.CompilerParams(
            dimension_semantics=("parallel", "parallel"),
            vmem_limit_bytes=_vmem_limit(blocks, temp_bytes=3 * _nbytes((bm, bn), F32))),
        name=name,
    )(a, w, *[arr for arr, _ in extras])


def _mm_kacc_body(a_ref, w_ref, r_ref, o_ref):
    part = jnp.dot(a_ref[...], w_ref[...], preferred_element_type=F32)

    @pl.when(pl.program_id(2) == 0)
    def _():
        o_ref[...] = r_ref[...] + part

    @pl.when(pl.program_id(2) > 0)
    def _():
        o_ref[...] += part


def _matmul_kacc(a, w, res, *, bm, bn, bk, name):
    m, kdim = a.shape
    n = w.shape[1]
    blocks = [_nbytes((bm, bk), a.dtype), _nbytes((bk, bn), w.dtype), 2 * _nbytes((bm, bn), F32)]
    return pl.pallas_call(
        _mm_kacc_body,
        grid=(m // bm, n // bn, kdim // bk),
        in_specs=[pl.BlockSpec((bm, bk), lambda i, j, k: (i, k)),
                  pl.BlockSpec((bk, bn), lambda i, j, k: (k, j)),
                  pl.BlockSpec((bm, bn), lambda i, j, k: (i, j))],
        out_specs=pl.BlockSpec((bm, bn), lambda i, j, k: (i, j)),
        out_shape=jax.ShapeDtypeStruct((m, n), F32),
        compiler_params=pltpu.CompilerParams(
            dimension_semantics=("parallel", "parallel", "arbitrary"),
            vmem_limit_bytes=_vmem_limit(blocks, temp_bytes=2 * _nbytes((bm, bn), F32))),
        name=name,
    )(a, w, res)


def _mm_wres_body(*refs, n_extra, epilogue, k_chunk):
    a_ref, w_ref = refs[0], refs[1]
    extra_refs = refs[2:2 + n_extra]
    o_ref = refs[2 + n_extra]
    wb_ref = refs[3 + n_extra]

    @pl.when(pl.program_id(1) == 0)
    def _():
        for c in range(wb_ref.shape[0] // k_chunk):
            rows = slice(c * k_chunk, (c + 1) * k_chunk)
            wb_ref[rows, :] = w_ref[rows, :].astype(BF16)

    acc = jnp.dot(a_ref[...], wb_ref[...], preferred_element_type=F32)
    o_ref[...] = epilogue(acc, *[r[...] for r in extra_refs]).astype(o_ref.dtype)


def _matmul_wres(a, w_stack, layer, *, n_cols, bm, bn, out_dtype, epilogue=_ep_identity, extras=(), name):
    m, kdim = a.shape
    in_specs = [pl.BlockSpec((bm, kdim), lambda j, i: (i, 0)),
                pl.BlockSpec((None, kdim, bn), lambda j, i: (layer, 0, j))]
    blocks = [_nbytes((bm, kdim), a.dtype), _nbytes((kdim, bn), F32), _nbytes((bm, bn), out_dtype)]
    for arr, off in extras:
        in_specs.append(pl.BlockSpec((bm, bn), lambda j, i, off=off: (i, off + j)))
        blocks.append(_nbytes((bm, bn), arr.dtype))
    return pl.pallas_call(
        functools.partial(_mm_wres_body, n_extra=len(extras), epilogue=epilogue, k_chunk=512),
        grid=(n_cols // bn, m // bm),
        in_specs=in_specs,
        out_specs=pl.BlockSpec((bm, bn), lambda j, i: (i, j)),
        out_shape=jax.ShapeDtypeStruct((m, n_cols), out_dtype),
        scratch_shapes=[pltpu.VMEM((kdim, bn), BF16)],
        compiler_params=pltpu.CompilerParams(
            dimension_semantics=("parallel", "arbitrary"),
            vmem_limit_bytes=_vmem_limit(blocks, scratch_bytes=_nbytes((kdim, bn), BF16),
                                         temp_bytes=3 * _nbytes((bm, bn), F32))),
        name=name,
    )(a, w_stack, *[arr for arr, _ in extras])


def _rope_body(ang_ref, cmask_ref, ssign_ref, cos_ref, sin_ref):
    ang = ang_ref[...]
    cos_ref[...] = jnp.cos(ang) * cmask_ref[...]
    sin_ref[...] = jnp.sin(ang) * ssign_ref[...]


def _rope_tables(ang128, *, tm=1024):
    m = ang128.shape[0]
    zeros = jnp.zeros((ROPE_HALF,), F32)
    ones = jnp.ones((ROPE_HALF,), F32)
    cmask = jnp.concatenate([ones, zeros, ones, zeros]).reshape(1, V7X_LANES)
    ssign = jnp.concatenate([-ones, zeros, ones, zeros]).reshape(1, V7X_LANES)
    row = pl.BlockSpec((tm, V7X_LANES), lambda i: (i, 0))
    vec = pl.BlockSpec((1, V7X_LANES), lambda i: (0, 0))
    return pl.pallas_call(
        _rope_body,
        grid=(m // tm,),
        in_specs=[row, vec, vec],
        out_specs=[row, row],
        out_shape=[jax.ShapeDtypeStruct((m, V7X_LANES), F32)] * 2,
        compiler_params=pltpu.CompilerParams(dimension_semantics=("parallel",)),
        name="rope_tables",
    )(ang128, cmask, ssign)


def _rotate(x, cos, sin):
    return x * cos + pltpu.roll(x, shift=V7X_LANES // 2, axis=1) * sin


def _split_bf16(x):
    hi = x.astype(BF16)
    r1 = x - hi.astype(F32)
    mid = r1.astype(BF16)
    lo = (r1 - mid.astype(F32)).astype(BF16)
    return hi, mid, lo


def _gate_body(i_ref, f_ref, bi_ref, bf_ref, bcol_ref, gcol_ref, grow_ref, *, chunk):
    li = GATE_SOFTCAP * jnp.tanh((i_ref[...] + bi_ref[...]) / GATE_SOFTCAP)
    z = GATE_SOFTCAP * jnp.tanh((f_ref[...] + bf_ref[...]) / GATE_SOFTCAP)
    lf = jnp.minimum(z, 0.0) - jnp.log(1.0 + jnp.exp(-jnp.abs(z)))
    row = lax.broadcasted_iota(jnp.int32, (chunk, chunk), 0)
    col = lax.broadcasted_iota(jnp.int32, (chunk, chunk), 1)
    tril = (col <= row).astype(BF16)
    b = sum(jnp.dot(tril, piece, preferred_element_type=F32) for piece in _split_bf16(lf))
    g = li - b
    bcol_ref[...] = b
    gcol_ref[...] = g
    grow_ref[...] = g.T[:V7X_SUBLANES, :]


def _mlstm_gates(small, b_i, b_f, *, i_blk, f_blk, chunk):
    m = small.shape[0]
    pad = lambda b: jnp.pad(b.astype(F32), (0, V7X_LANES - b.shape[0])).reshape(1, V7X_LANES)
    col = pl.BlockSpec((chunk, V7X_LANES), lambda c: (c, 0))
    vec = pl.BlockSpec((1, V7X_LANES), lambda c: (0, 0))
    return pl.pallas_call(
        functools.partial(_gate_body, chunk=chunk),
        grid=(m // chunk,),
        in_specs=[pl.BlockSpec((chunk, V7X_LANES), lambda c: (c, i_blk)),
                  pl.BlockSpec((chunk, V7X_LANES), lambda c: (c, f_blk)),
                  vec, vec],
        out_specs=[col, col, pl.BlockSpec((V7X_SUBLANES, chunk), lambda c: (0, c))],
        out_shape=[jax.ShapeDtypeStruct((m, V7X_LANES), F32),
                   jax.ShapeDtypeStruct((m, V7X_LANES), F32),
                   jax.ShapeDtypeStruct((V7X_SUBLANES, m), F32)],
        compiler_params=pltpu.CompilerParams(dimension_semantics=("parallel",)),
        name="mlstm_gates",
    )(small, small, pad(b_i), pad(b_f))


def _mlstm_body(q_ref, k_ref, v_ref, o_ref, bcol_ref, gcol_ref, grow_ref, hn_ref, out_ref,
                c_ref, n_ref, m_ref, *, chunk):
    head = pl.program_id(1)

    @pl.when(pl.program_id(2) == 0)
    def _():
        c_ref[...] = jnp.zeros_like(c_ref)
        n_ref[...] = jnp.zeros_like(n_ref)
        m_ref[...] = jnp.zeros_like(m_ref)

    lane_sel = lax.broadcasted_iota(jnp.int32, (1, V7X_LANES), 1) == head
    bcol = jnp.sum(jnp.where(lane_sel, bcol_ref[...], 0.0), axis=-1, keepdims=True)
    gcol = jnp.sum(jnp.where(lane_sel, gcol_ref[...], 0.0), axis=-1, keepdims=True)
    sub_sel = lax.broadcasted_iota(jnp.int32, (V7X_SUBLANES, 1), 0) == head
    grow = jnp.sum(jnp.where(sub_sel, grow_ref[...], 0.0), axis=0, keepdims=True)

    m_prev = m_ref[...]
    row = lax.broadcasted_iota(jnp.int32, (chunk, chunk), 0)
    col = lax.broadcasted_iota(jnp.int32, (chunk, chunk), 1)
    dmat = jnp.where(col <= row, bcol + grow, -jnp.inf)
    inter = bcol + m_prev
    mj = jnp.maximum(inter, jnp.max(dmat, axis=-1, keepdims=True))
    w_inter = jnp.exp(inter - mj)

    q = q_ref[...]
    k = k_ref[...]
    v = v_ref[...]
    scale = M_DQK ** -0.5
    qk = lax.dot_general(q, k, (((1,), (1,)), ((), ())), preferred_element_type=F32) * scale
    p = jnp.exp(dmat - mj) * qk
    cq = jnp.dot(q, c_ref[...].astype(BF16), preferred_element_type=F32) * scale
    num = w_inter * cq + jnp.dot(p.astype(BF16), v, preferred_element_type=F32)
    qn = jnp.sum(q.astype(F32) * n_ref[...], axis=-1, keepdims=True) * scale
    nq = w_inter * qn + jnp.sum(p, axis=-1, keepdims=True)
    hv = num / jnp.maximum(jnp.abs(nq), jnp.exp(-mj))

    ms = jnp.mean(hv * hv, axis=-1, keepdims=True)
    hn = hv * lax.rsqrt(ms + EPS) * hn_ref[...]
    out_ref[...] = (hn * jax.nn.sigmoid(o_ref[...].astype(F32))).astype(out_ref.dtype)

    b_last = bcol[chunk - 1:chunk, :]
    acol = b_last + gcol
    m_new = jnp.maximum(b_last + m_prev, jnp.max(acol, axis=0, keepdims=True))
    decay = jnp.exp(b_last + m_prev - m_new)
    kw = k.astype(F32) * jnp.exp(acol - m_new)
    c_ref[...] = decay * c_ref[...] + lax.dot_general(
        kw.astype(BF16), v, (((0,), (0,)), ((), ())), preferred_element_type=F32)
    n_ref[...] = decay * n_ref[...] + jnp.sum(kw, axis=0, keepdims=True)
    m_ref[...] = m_new


def _mlstm(proj, bcol, gcol, grow, head_gain, *, batch, seq, chunk):
    m = proj.shape[0]
    nc = seq // chunk
    k_blk0 = (M_HEADS * M_DQK) // M_DQK
    v_blk0 = (2 * M_HEADS * M_DQK) // M_DV
    o_blk0 = v_blk0 + M_HEADS
    rows = lambda b, h, c: b * nc + c
    return pl.pallas_call(
        functools.partial(_mlstm_body, chunk=chunk),
        grid=(batch, M_HEADS, nc),
        in_specs=[
            pl.BlockSpec((chunk, M_DQK), lambda b, h, c: (rows(b, h, c), h)),
            pl.BlockSpec((chunk, M_DQK), lambda b, h, c: (rows(b, h, c), k_blk0 + h)),
            pl.BlockSpec((chunk, M_DV), lambda b, h, c: (rows(b, h, c), v_blk0 + h)),
            pl.BlockSpec((chunk, M_DV), lambda b, h, c: (rows(b, h, c), o_blk0 + h)),
            pl.BlockSpec((chunk, V7X_LANES), lambda b, h, c: (rows(b, h, c), 0)),
            pl.BlockSpec((chunk, V7X_LANES), lambda b, h, c: (rows(b, h, c), 0)),
            pl.BlockSpec((V7X_SUBLANES, chunk), lambda b, h, c: (0, rows(b, h, c))),
            pl.BlockSpec((1, M_DV), lambda b, h, c: (0, h)),
        ],
        out_specs=pl.BlockSpec((chunk, M_DV), lambda b, h, c: (rows(b, h, c), h)),
        out_shape=jax.ShapeDtypeStruct((m, M_HEADS * M_DV), BF16),
        scratch_shapes=[pltpu.VMEM((M_DQK, M_DV), F32),
                        pltpu.VMEM((1, M_DQK), F32),
                        pltpu.VMEM((1, 1), F32)],
        compiler_params=pltpu.CompilerParams(
            dimension_semantics=("parallel", "parallel", "arbitrary")),
        name="mlstm",
    )(proj, proj, proj, proj, bcol, gcol, grow, head_gain.reshape(1, M_HEADS * M_DV))


def _row_rmsnorm(a, gain):
    ms = jnp.mean(a * a, axis=-1, keepdims=True)
    return (a * lax.rsqrt(ms + EPS) * gain).astype(BF16)


def _qproj_body(a_ref, an_ref, w_ref, gq_ref, cos_ref, sin_ref, o_ref, *, heads_per_block):
    a = _row_rmsnorm(a_ref[...], an_ref[...])
    acc = jnp.dot(a, w_ref[...], preferred_element_type=F32)
    cos = cos_ref[...]
    sin = sin_ref[...]
    gq = gq_ref[...]
    for hh in range(heads_per_block):
        blk = acc[:, hh * A_HEAD_PAD:(hh + 1) * A_HEAD_PAD]
        ss = jnp.sum(blk * blk, axis=-1, keepdims=True)
        qn = blk * lax.rsqrt(ss * (1.0 / A_QK) + EPS) * gq
        rope = _rotate(qn[:, V7X_LANES:], cos, sin)
        o_ref[:, hh * A_HEAD_PAD:hh * A_HEAD_PAD + V7X_LANES] = qn[:, :V7X_LANES].astype(o_ref.dtype)
        o_ref[:, hh * A_HEAD_PAD + V7X_LANES:(hh + 1) * A_HEAD_PAD] = rope.astype(o_ref.dtype)


def _qproj(small, a_norm, w_q, gq, cos, sin, *, rank, bm=512, heads_per_block=4):
    m = small.shape[0]
    bn = heads_per_block * A_HEAD_PAD
    n = w_q.shape[1]
    rowvec = pl.BlockSpec((bm, V7X_LANES), lambda i, j: (i, 0))
    return pl.pallas_call(
        functools.partial(_qproj_body, heads_per_block=heads_per_block),
        grid=(m // bm, n // bn),
        in_specs=[pl.BlockSpec((bm, rank), lambda i, j: (i, 0)),
                  pl.BlockSpec((1, rank), lambda i, j: (0, 0)),
                  pl.BlockSpec((rank, bn), lambda i, j: (0, j)),
                  pl.BlockSpec((1, A_HEAD_PAD), lambda i, j: (0, 0)),
                  rowvec, rowvec],
        out_specs=pl.BlockSpec((bm, bn), lambda i, j: (i, j)),
        out_shape=jax.ShapeDtypeStruct((m, n), BF16),
        compiler_params=pltpu.CompilerParams(
            dimension_semantics=("parallel", "parallel"),
            vmem_limit_bytes=_vmem_limit([_nbytes((bm, rank), F32), _nbytes((rank, bn), BF16),
                                          _nbytes((bm, bn), BF16)],
                                         temp_bytes=4 * _nbytes((bm, bn), F32))),
        name="mla_qproj",
    )(small, a_norm.reshape(1, rank), w_q, gq, cos, sin)


def _kvproj_body(a_ref, an_ref, w_ref, kr_ref, gkn_ref, gkr_ref, cos_ref, sin_ref, k_ref, v_ref, *,
                 heads_per_block):
    a = _row_rmsnorm(a_ref[...], an_ref[...])
    acc = jnp.dot(a, w_ref[...], preferred_element_type=F32)
    kr = kr_ref[...]
    ss_rope = jnp.sum(kr * kr, axis=-1, keepdims=True)
    k_rot = _rotate(kr * gkr_ref[...], cos_ref[...], sin_ref[...])
    gkn = gkn_ref[...]
    for hh in range(heads_per_block):
        kn = acc[:, hh * A_NOPE:(hh + 1) * A_NOPE]
        ss = jnp.sum(kn * kn, axis=-1, keepdims=True) + ss_rope
        rs = lax.rsqrt(ss * (1.0 / A_QK) + EPS)
        k_ref[:, hh * A_HEAD_PAD:hh * A_HEAD_PAD + V7X_LANES] = (kn * rs * gkn).astype(k_ref.dtype)
        k_ref[:, hh * A_HEAD_PAD + V7X_LANES:(hh + 1) * A_HEAD_PAD] = (k_rot * rs).astype(k_ref.dtype)
    v0 = heads_per_block * A_NOPE
    ones = jnp.ones((v_ref.shape[0], A_DV), v_ref.dtype)
    for hh in range(heads_per_block):
        v_ref[:, hh * A_V_PAD:hh * A_V_PAD + A_DV] = acc[:, v0 + hh * A_DV:v0 + (hh + 1) * A_DV].astype(v_ref.dtype)
        v_ref[:, hh * A_V_PAD + A_DV:(hh + 1) * A_V_PAD] = ones


def _kvproj(small, a_norm, w_kv, gkn, gkr, cos, sin, *, a_blk, rank, rope_blk, bm=512, heads_per_block=4):
    m = small.shape[0]
    bn = heads_per_block * (A_NOPE + A_DV)
    n_blocks = w_kv.shape[1] // bn
    rowvec = pl.BlockSpec((bm, V7X_LANES), lambda i, j: (i, 0))
    lanevec = pl.BlockSpec((1, V7X_LANES), lambda i, j: (0, 0))
    return pl.pallas_call(
        functools.partial(_kvproj_body, heads_per_block=heads_per_block),
        grid=(m // bm, n_blocks),
        in_specs=[pl.BlockSpec((bm, rank), lambda i, j: (i, a_blk)),
                  pl.BlockSpec((1, rank), lambda i, j: (0, 0)),
                  pl.BlockSpec((rank, bn), lambda i, j: (0, j)),
                  pl.BlockSpec((bm, V7X_LANES), lambda i, j: (i, rope_blk)),
                  lanevec, lanevec, rowvec, rowvec],
        out_specs=[pl.BlockSpec((bm, heads_per_block * A_HEAD_PAD), lambda i, j: (i, j)),
                   pl.BlockSpec((bm, heads_per_block * A_V_PAD), lambda i, j: (i, j))],
        out_shape=[jax.ShapeDtypeStruct((m, A_HEADS * A_HEAD_PAD), BF16),
                   jax.ShapeDtypeStruct((m, A_HEADS * A_V_PAD), BF16)],
        compiler_params=pltpu.CompilerParams(
            dimension_semantics=("parallel", "parallel"),
            vmem_limit_bytes=_vmem_limit([_nbytes((bm, rank), F32), _nbytes((rank, bn), BF16),
                                          _nbytes((bm, bn), BF16), _nbytes((bm, bn), BF16)],
                                         temp_bytes=4 * _nbytes((bm, bn), F32))),
        name="mla_kvproj",
    )(small, a_norm.reshape(1, rank), w_kv, small, gkn, gkr, cos, sin)


FLASH_HEADS = 2


def _flash_body(q_ref, k_ref, v_ref, o_ref, s_ref, mx_ref, acc_ref, *, tq, tkc):
    i = pl.program_id(2)
    n_slab = tkc // V7X_LANES
    n_unmasked = (i * tq) // tkc
    n_chunks = ((i + 1) * tq + tkc - 1) // tkc
    mx_ref[...] = jnp.full_like(mx_ref, -jnp.inf)
    acc_ref[...] = jnp.zeros_like(acc_ref)

    def scores(c, masked):
        rows = pl.ds(pl.multiple_of(c * tkc, tkc), tkc)
        for e in range(FLASH_HEADS):
            s = lax.dot_general(q_ref[:, e * A_HEAD_PAD:(e + 1) * A_HEAD_PAD],
                                k_ref[rows, e * A_HEAD_PAD:(e + 1) * A_HEAD_PAD],
                                (((1,), (1,)), ((), ())), preferred_element_type=F32)
            if masked:
                row = lax.broadcasted_iota(jnp.int32, (tq, tkc), 0)
                col = lax.broadcasted_iota(jnp.int32, (tq, tkc), 1)
                s = jnp.where(col + c * tkc <= row + i * tq, s, -jnp.inf)
            s_ref[e, c] = s
            mx = mx_ref[e]
            for t in range(n_slab):
                mx = jnp.maximum(mx, s[:, t * V7X_LANES:(t + 1) * V7X_LANES])
            mx_ref[e] = mx

    def unmasked_chunk(c, carry):
        scores(c, False)
        return carry

    def masked_chunk(c, carry):
        scores(c, True)
        return carry

    lax.fori_loop(0, n_unmasked, unmasked_chunk, 0)
    lax.fori_loop(n_unmasked, n_chunks, masked_chunk, 0)

    for e in range(FLASH_HEADS):
        mx_ref[e] = jnp.broadcast_to(jnp.max(mx_ref[e], axis=-1, keepdims=True), (tq, V7X_LANES))

    def probs_chunk(c, carry):
        rows = pl.ds(pl.multiple_of(c * tkc, tkc), tkc)
        for e in range(FLASH_HEADS):
            s = s_ref[e, c]
            mx = mx_ref[e]
            p = jnp.concatenate([jnp.exp2(s[:, t * V7X_LANES:(t + 1) * V7X_LANES] - mx) for t in range(n_slab)],
                                axis=-1).astype(BF16)
            acc_ref[e] += jnp.dot(p, v_ref[rows, e * A_V_PAD:(e + 1) * A_V_PAD], preferred_element_type=F32)
        return carry

    lax.fori_loop(0, n_chunks, probs_chunk, 0)
    for e in range(FLASH_HEADS):
        acc = acc_ref[e]
        o_ref[:, e * A_DV:(e + 1) * A_DV] = (acc[:, :A_DV] / acc[:, A_DV:]).astype(o_ref.dtype)


def _flash_attention(q, k, v1, *, batch, seq, tq=512, tkc=1024):
    m = q.shape[0]
    nq = seq // tq
    hb = FLASH_HEADS
    blocks = [_nbytes((tq, hb * A_HEAD_PAD), BF16), _nbytes((seq, hb * A_HEAD_PAD), BF16),
              _nbytes((seq, hb * A_V_PAD), BF16), _nbytes((tq, hb * A_DV), BF16)]
    scratch_shapes = [pltpu.VMEM((hb, seq // tkc, tq, tkc), F32),
                      pltpu.VMEM((hb, tq, V7X_LANES), F32),
                      pltpu.VMEM((hb, tq, A_V_PAD), F32)]
    scratch = _nbytes((hb, tq, seq), F32) + _nbytes((hb, tq, V7X_LANES), F32) + _nbytes((hb, tq, A_V_PAD), F32)
    return pl.pallas_call(
        functools.partial(_flash_body, tq=tq, tkc=tkc),
        grid=(batch, A_HEADS // hb, nq),
        in_specs=[
            pl.BlockSpec((tq, hb * A_HEAD_PAD), lambda b, h, i: (b * nq + i, h)),
            pl.BlockSpec((seq, hb * A_HEAD_PAD), lambda b, h, i: (b, h)),
            pl.BlockSpec((seq, hb * A_V_PAD), lambda b, h, i: (b, h)),
        ],
        out_specs=pl.BlockSpec((tq, hb * A_DV), lambda b, h, i: (b * nq + i, h)),
        out_shape=jax.ShapeDtypeStruct((m, A_HEADS * A_DV), BF16),
        scratch_shapes=scratch_shapes,
        compiler_params=pltpu.CompilerParams(
            dimension_semantics=("parallel", "parallel", "parallel"),
            vmem_limit_bytes=_vmem_limit(blocks, scratch_bytes=scratch,
                                         temp_bytes=hb * 3 * _nbytes((tq, tkc), F32))),
        name="mla_flash",
    )(q, k, v1)


def _rope_lane_layout(x):
    z = jnp.zeros(x.shape[:-1] + (ROPE_HALF,), x.dtype)
    return jnp.concatenate([x[..., :ROPE_HALF], z, x[..., ROPE_HALF:], z], axis=-1)


def _head_lane_layout(x):
    return jnp.concatenate([x[..., :A_NOPE], _rope_lane_layout(x[..., A_NOPE:])], axis=-1)


def _layer(xf, cos, sin, layer, norm_mix, w_in_stack, b_igate, b_fgate, m_head_norm, w_out_m_stack, q_a_norm,
           w_uq, kv_a_norm, w_ukv, qk_norm_q, qk_norm_k, w_out_a_stack, w_out_stack, norm_mlp, w_up_stack,
           w_down, *, batch, seq):
    d_model = xf.shape[1]
    q_lora = w_uq.shape[0]
    kv_lora = w_ukv.shape[0]
    m_qk_w = M_HEADS * M_DQK
    m_v_w = M_HEADS * M_DV
    w_in = w_in_stack[layer]
    o_i = 2 * m_qk_w + 2 * m_v_w
    o_f = o_i + M_HEADS
    o_cq = o_f + M_HEADS
    o_kva = o_cq + q_lora
    o_ga = o_kva + kv_lora + A_ROPE
    w_main = jnp.concatenate([w_in[:, :o_i], w_in[:, o_ga:]], axis=1).astype(BF16)
    zcols = lambda n: jnp.zeros((d_model, n), w_in.dtype)
    gate_pad = V7X_LANES - M_HEADS
    w_small = jnp.concatenate([
        w_in[:, o_cq:o_kva],
        w_in[:, o_kva:o_kva + kv_lora],
        _rope_lane_layout(w_in[:, o_kva + kv_lora:o_ga]),
        w_in[:, o_i:o_f], zcols(gate_pad),
        w_in[:, o_f:o_cq], zcols(gate_pad),
    ], axis=1).astype(BF16)
    kv_blk = q_lora // kv_lora
    rope_blk = (q_lora + kv_lora) // V7X_LANES
    i_blk = rope_blk + 1
    f_blk = i_blk + 1

    w_q = _head_lane_layout(w_uq.reshape(q_lora, A_HEADS, A_QK)).reshape(q_lora, A_HEADS * A_HEAD_PAD)
    w_q = w_q.astype(BF16)
    gq = (_head_lane_layout(qk_norm_q) * (A_QK ** -0.5 * LOG2E)).reshape(1, A_HEAD_PAD)
    hpb = 4
    w_kv4 = w_ukv.reshape(kv_lora, A_HEADS // hpb, hpb, A_NOPE + A_DV)
    w_kv = jnp.concatenate([w_kv4[..., :A_NOPE].reshape(kv_lora, A_HEADS // hpb, hpb * A_NOPE),
                            w_kv4[..., A_NOPE:].reshape(kv_lora, A_HEADS // hpb, hpb * A_DV)], axis=-1)
    w_kv = w_kv.reshape(kv_lora, A_HEADS * (A_NOPE + A_DV)).astype(BF16)
    gkn = qk_norm_k[:A_NOPE].reshape(1, V7X_LANES)
    gkr = _rope_lane_layout(qk_norm_k[A_NOPE:]).reshape(1, V7X_LANES)

    h = _rmsnorm(xf, norm_mix)
    proj = _matmul(h, w_main, bm=1024, bn=1024, out_dtype=BF16, name="in_proj_main")
    small = _matmul(h, w_small, bm=1024, bn=w_small.shape[1] // 3, out_dtype=F32, name="in_proj_small")

    bcol, gcol, grow = _mlstm_gates(small, b_igate, b_fgate, i_blk=i_blk, f_blk=f_blk, chunk=MLSTM_CHUNK)
    hm = _mlstm(proj, bcol, gcol, grow, m_head_norm, batch=batch, seq=seq, chunk=MLSTM_CHUNK)

    qh = _qproj(small, q_a_norm, w_q, gq, cos, sin, rank=q_lora, heads_per_block=hpb)
    kh, vh = _kvproj(small, kv_a_norm, w_kv, gkn, gkr, cos, sin, a_blk=kv_blk, rank=kv_lora,
                     rope_blk=rope_blk, heads_per_block=hpb)
    att = _flash_attention(qh, kh, vh, batch=batch, seq=seq)

    bn = 512
    ga_blk0 = o_i // bn
    gb_blk0 = ga_blk0 + d_model // bn
    t = _matmul_wres(hm, w_out_m_stack, layer, n_cols=d_model, bm=1024, bn=bn, out_dtype=F32,
                     epilogue=_ep_gate, extras=[(proj, ga_blk0)], name="out_proj_mlstm")
    merged = _matmul_wres(att, w_out_a_stack, layer, n_cols=d_model, bm=1024, bn=bn, out_dtype=BF16,
                          epilogue=_ep_gate_add, extras=[(proj, gb_blk0), (t, 0)], name="out_proj_mla")
    xf = _matmul_wres(merged, w_out_stack, layer, n_cols=d_model, bm=1024, bn=bn, out_dtype=F32,
                      epilogue=_ep_residual, extras=[(xf, 0)], name="out_proj")

    h2 = _rmsnorm(xf, norm_mlp)
    u = _matmul_wres(h2, w_up_stack, layer, n_cols=w_up_stack.shape[2], bm=1024, bn=bn, out_dtype=BF16,
                     epilogue=_ep_relu2, name="mlp_up")
    return _matmul_kacc(u, w_down.astype(BF16), xf, bm=1024, bn=1024, bk=4096, name="mlp_down")


def kernel(x, positions, norm_mix, w_in, b_igate, b_fgate, m_head_norm, w_out_m, q_a_norm, w_uq,
           kv_a_norm, w_ukv, qk_norm_q, qk_norm_k, w_out_a, w_out, norm_mlp, w_up, w_down):
    batch, seq, d_model = x.shape
    depth = w_in.shape[0]
    m = batch * seq
    inv_freq = ROPE_THETA ** (-jnp.arange(0, A_ROPE, 2, dtype=F32) / A_ROPE)
    ang = positions.astype(F32).reshape(m, 1) * inv_freq
    zeros = jnp.zeros_like(ang)
    cos, sin = _rope_tables(jnp.concatenate([ang, zeros, ang, zeros], axis=-1))
    xf = x.reshape(m, d_model)
    for l in range(depth):
        xf = _layer(xf, cos, sin, l, norm_mix[l], w_in, b_igate[l], b_fgate[l], m_head_norm[l], w_out_m,
                    q_a_norm[l], w_uq[l], kv_a_norm[l], w_ukv[l], qk_norm_q[l], qk_norm_k[l], w_out_a,
                    w_out, norm_mlp[l], w_up, w_down[l], batch=batch, seq=seq)
    return xf.reshape(batch, seq, d_model)
```

```python
import functools
import math

import jax
import jax.numpy as jnp
from jax import lax
from jax.experimental import pallas as pl
from jax.experimental.pallas import tpu as pltpu

F32 = jnp.float32
BF16 = jnp.bfloat16

M_HEADS = 8
M_DQK = 256
M_DV = 512
GATE_SOFTCAP = 15.0
A_HEADS = 32
A_NOPE = 128
A_ROPE = 64
A_QK = A_NOPE + A_ROPE
A_DV = 128
ROPE_THETA = 10000.0
EPS = 1e-6

V7X_LANES = 128
V7X_SUBLANES = 8
V7X_VMEM_BYTES = 64 * 1024 * 1024
V7X_VMEM_REQUEST_CAP = V7X_VMEM_BYTES - 6 * 1024 * 1024

A_HEAD_PAD = 2 * V7X_LANES
A_V_PAD = 2 * A_DV
ROPE_HALF = A_ROPE // 2
MLSTM_CHUNK = 256
LOG2E = math.log2(math.e)


def _vmem_limit(block_bytes, scratch_bytes=0, temp_bytes=0):
    need = 2 * sum(block_bytes) + scratch_bytes + temp_bytes + (2 << 20)
    return int(min(max(need, 16 << 20), V7X_VMEM_REQUEST_CAP))


def _nbytes(shape, dtype):
    return math.prod(shape) * jnp.dtype(dtype).itemsize


def _rmsnorm_body(x_ref, g_ref, o_ref):
    x = x_ref[...]
    ms = jnp.mean(x * x, axis=-1, keepdims=True)
    o_ref[...] = (x * lax.rsqrt(ms + EPS) * g_ref[...]).astype(o_ref.dtype)


def _rmsnorm(x, g, *, tm=256):
    m, d = x.shape
    return pl.pallas_call(
        _rmsnorm_body,
        grid=(m // tm,),
        in_specs=[pl.BlockSpec((tm, d), lambda i: (i, 0)),
                  pl.BlockSpec((1, d), lambda i: (0, 0))],
        out_specs=pl.BlockSpec((tm, d), lambda i: (i, 0)),
        out_shape=jax.ShapeDtypeStruct((m, d), BF16),
        compiler_params=pltpu.CompilerParams(
            dimension_semantics=("parallel",),
            vmem_limit_bytes=_vmem_limit([_nbytes((tm, d), F32), _nbytes((tm, d), BF16)],
                                         temp_bytes=2 * _nbytes((tm, d), F32))),
        name="rmsnorm",
    )(x, g.reshape(1, d))


def _ep_identity(acc):
    return acc


def _ep_relu2(acc):
    return jnp.square(jnp.maximum(acc, 0.0))


def _ep_gate(acc, g):
    return jax.nn.sigmoid(g.astype(F32)) * acc


def _ep_gate_add(acc, g, t):
    return t + jax.nn.sigmoid(g.astype(F32)) * acc


def _ep_residual(acc, r):
    return r + acc


def _mm_body(*refs, n_extra, epilogue):
    a_ref, w_ref = refs[0], refs[1]
    extra_refs = refs[2:2 + n_extra]
    o_ref = refs[2 + n_extra]
    acc = jnp.dot(a_ref[...], w_ref[...], preferred_element_type=F32)
    o_ref[...] = epilogue(acc, *[r[...] for r in extra_refs]).astype(o_ref.dtype)


def _matmul(a, w, *, bm, bn, out_dtype, epilogue=_ep_identity, extras=(), name):
    m, kdim = a.shape
    n = w.shape[1]
    in_specs = [pl.BlockSpec((bm, kdim), lambda i, j: (i, 0)),
                pl.BlockSpec((kdim, bn), lambda i, j: (0, j))]
    blocks = [_nbytes((bm, kdim), a.dtype), _nbytes((kdim, bn), w.dtype), _nbytes((bm, bn), out_dtype)]
    for arr, off in extras:
        in_specs.append(pl.BlockSpec((bm, bn), lambda i, j, off=off: (i, off + j)))
        blocks.append(_nbytes((bm, bn), arr.dtype))
    return pl.pallas_call(
        functools.partial(_mm_body, n_extra=len(extras), epilogue=epilogue),
        grid=(m // bm, n // bn),
        in_specs=in_specs,
        out_specs=pl.BlockSpec((bm, bn), lambda i, j: (i, j)),
        out_shape=jax.ShapeDtypeStruct((m, n), out_dtype),
        compiler_params=pltpu.CompilerParams(
            dimension_semantics=("parallel", "parallel"),
            vmem_limit_bytes=_vmem_limit(blocks, temp_bytes=3 * _nbytes((bm, bn), F32))),
        name=name,
    )(a, w, *[arr for arr, _ in extras])


def _mm_kacc_body(a_ref, w_ref, r_ref, o_ref):
    part = jnp.dot(a_ref[...], w_ref[...], preferred_element_type=F32)

    @pl.when(pl.program_id(2) == 0)
    def _():
        o_ref[...] = r_ref[...] + part

    @pl.when(pl.program_id(2) > 0)
    def _():
        o_ref[...] += part


def _matmul_kacc(a, w, res, *, bm, bn, bk, name):
    m, kdim = a.shape
    n = w.shape[1]
    blocks = [_nbytes((bm, bk), a.dtype), _nbytes((bk, bn), w.dtype), 2 * _nbytes((bm, bn), F32)]
    return pl.pallas_call(
        _mm_kacc_body,
        grid=(m // bm, n // bn, kdim // bk),
        in_specs=[pl.BlockSpec((bm, bk), lambda i, j, k: (i, k)),
                  pl.BlockSpec((bk, bn), lambda i, j, k: (k, j)),
                  pl.BlockSpec((bm, bn), lambda i, j, k: (i, j))],
        out_specs=pl.BlockSpec((bm, bn), lambda i, j, k: (i, j)),
        out_shape=jax.ShapeDtypeStruct((m, n), F32),
        compiler_params=pltpu.CompilerParams(
            dimension_semantics=("parallel", "parallel", "arbitrary"),
            vmem_limit_bytes=_vmem_limit(blocks, temp_bytes=2 * _nbytes((bm, bn), F32))),
        name=name,
    )(a, w, res)


def _mm_wres_body(*refs, n_extra, epilogue, k_chunk):
    a_ref, w_ref = refs[0], refs[1]
    extra_refs = refs[2:2 + n_extra]
    o_ref = refs[2 + n_extra]
    wb_ref = refs[3 + n_extra]

    @pl.when(pl.program_id(1) == 0)
    def _():
        for c in range(wb_ref.shape[0] // k_chunk):
            rows = slice(c * k_chunk, (c + 1) * k_chunk)
            wb_ref[rows, :] = w_ref[rows, :].astype(BF16)

    acc = jnp.dot(a_ref[...], wb_ref[...], preferred_element_type=F32)
    o_ref[...] = epilogue(acc, *[r[...] for r in extra_refs]).astype(o_ref.dtype)


def _matmul_wres(a, w_stack, layer, *, n_cols, bm, bn, out_dtype, epilogue=_ep_identity, extras=(), name):
    m, kdim = a.shape
    in_specs = [pl.BlockSpec((bm, kdim), lambda j, i: (i, 0)),
                pl.BlockSpec((None, kdim, bn), lambda j, i: (layer, 0, j))]
    blocks = [_nbytes((bm, kdim), a.dtype), _nbytes((kdim, bn), F32), _nbytes((bm, bn), out_dtype)]
    for arr, off in extras:
        in_specs.append(pl.BlockSpec((bm, bn), lambda j, i, off=off: (i, off + j)))
        blocks.append(_nbytes((bm, bn), arr.dtype))
    return pl.pallas_call(
        functools.partial(_mm_wres_body, n_extra=len(extras), epilogue=epilogue, k_chunk=512),
        grid=(n_cols // bn, m // bm),
        in_specs=in_specs,
        out_specs=pl.BlockSpec((bm, bn), lambda j, i: (i, j)),
        out_shape=jax.ShapeDtypeStruct((m, n_cols), out_dtype),
        scratch_shapes=[pltpu.VMEM((kdim, bn), BF16)],
        compiler_params=pltpu.CompilerParams(
            dimension_semantics=("parallel", "arbitrary"),
            vmem_limit_bytes=_vmem_limit(blocks, scratch_bytes=_nbytes((kdim, bn), BF16),
                                         temp_bytes=3 * _nbytes((bm, bn), F32))),
        name=name,
    )(a, w_stack, *[arr for arr, _ in extras])


def _rope_body(ang_ref, cmask_ref, lo_sign_ref, hi_sign_ref, cos_ref, sin_lo_ref, sin_hi_ref):
    ang = ang_ref[...]
    sin = jnp.sin(ang)
    cos_ref[...] = jnp.cos(ang) * cmask_ref[...]
    sin_lo_ref[...] = sin * lo_sign_ref[...]
    sin_hi_ref[...] = sin * hi_sign_ref[...]


def _rope_tables(ang128, *, tm=1024):
    m = ang128.shape[0]
    lane = jnp.arange(V7X_LANES)
    lo = (lane < ROPE_HALF).astype(F32).reshape(1, V7X_LANES)
    hi = jnp.logical_and(lane >= ROPE_HALF, lane < A_ROPE).astype(F32).reshape(1, V7X_LANES)
    row = pl.BlockSpec((tm, V7X_LANES), lambda i: (i, 0))
    vec = pl.BlockSpec((1, V7X_LANES), lambda i: (0, 0))
    return pl.pallas_call(
        _rope_body,
        grid=(m // tm,),
        in_specs=[row, vec, vec, vec],
        out_specs=[row, row, row],
        out_shape=[jax.ShapeDtypeStruct((m, V7X_LANES), F32)] * 3,
        compiler_params=pltpu.CompilerParams(dimension_semantics=("parallel",)),
        name="rope_tables",
    )(ang128, lo + hi, -lo, hi)


def _rotate(x, rope):
    cos, sin_lo, sin_hi = rope
    hi_on_lo = pltpu.roll(x, shift=V7X_LANES - ROPE_HALF, axis=1)
    lo_on_hi = pltpu.roll(x, shift=ROPE_HALF, axis=1)
    return x * cos + hi_on_lo * sin_lo + lo_on_hi * sin_hi


def _split_bf16(x):
    hi = x.astype(BF16)
    r1 = x - hi.astype(F32)
    mid = r1.astype(BF16)
    lo = (r1 - mid.astype(F32)).astype(BF16)
    return hi, mid, lo


def _gate_body(i_ref, f_ref, bi_ref, bf_ref, bcol_ref, gcol_ref, grow_ref, *, chunk):
    li = GATE_SOFTCAP * jnp.tanh((i_ref[...] + bi_ref[...]) / GATE_SOFTCAP)
    z = GATE_SOFTCAP * jnp.tanh((f_ref[...] + bf_ref[...]) / GATE_SOFTCAP)
    lf = jnp.minimum(z, 0.0) - jnp.log(1.0 + jnp.exp(-jnp.abs(z)))
    row = lax.broadcasted_iota(jnp.int32, (chunk, chunk), 0)
    col = lax.broadcasted_iota(jnp.int32, (chunk, chunk), 1)
    tril = (col <= row).astype(BF16)
    b = sum(jnp.dot(tril, piece, preferred_element_type=F32) for piece in _split_bf16(lf))
    g = li - b
    bcol_ref[...] = b
    gcol_ref[...] = g
    grow_ref[...] = g.T[:V7X_SUBLANES, :]


def _mlstm_gates(small, b_i, b_f, *, i_blk, f_blk, chunk):
    m = small.shape[0]
    pad = lambda b: jnp.pad(b.astype(F32), (0, V7X_LANES - b.shape[0])).reshape(1, V7X_LANES)
    col = pl.BlockSpec((chunk, V7X_LANES), lambda c: (c, 0))
    vec = pl.BlockSpec((1, V7X_LANES), lambda c: (0, 0))
    return pl.pallas_call(
        functools.partial(_gate_body, chunk=chunk),
        grid=(m // chunk,),
        in_specs=[pl.BlockSpec((chunk, V7X_LANES), lambda c: (c, i_blk)),
                  pl.BlockSpec((chunk, V7X_LANES), lambda c: (c, f_blk)),
                  vec, vec],
        out_specs=[col, col, pl.BlockSpec((V7X_SUBLANES, chunk), lambda c: (0, c))],
        out_shape=[jax.ShapeDtypeStruct((m, V7X_LANES), F32),
                   jax.ShapeDtypeStruct((m, V7X_LANES), F32),
                   jax.ShapeDtypeStruct((V7X_SUBLANES, m), F32)],
        compiler_params=pltpu.CompilerParams(dimension_semantics=("parallel",)),
        name="mlstm_gates",
    )(small, small, pad(b_i), pad(b_f))


def _mlstm_body(q_ref, k_ref, v_ref, o_ref, bcol_ref, gcol_ref, grow_ref, hn_ref, out_ref,
                c_ref, n_ref, m_ref, *, chunk):
    head = pl.program_id(1)

    @pl.when(pl.program_id(2) == 0)
    def _():
        c_ref[...] = jnp.zeros_like(c_ref)
        n_ref[...] = jnp.zeros_like(n_ref)
        m_ref[...] = jnp.zeros_like(m_ref)

    lane_sel = lax.broadcasted_iota(jnp.int32, (1, V7X_LANES), 1) == head
    bcol = jnp.sum(jnp.where(lane_sel, bcol_ref[...], 0.0), axis=-1, keepdims=True)
    gcol = jnp.sum(jnp.where(lane_sel, gcol_ref[...], 0.0), axis=-1, keepdims=True)
    sub_sel = lax.broadcasted_iota(jnp.int32, (V7X_SUBLANES, 1), 0) == head
    grow = jnp.sum(jnp.where(sub_sel, grow_ref[...], 0.0), axis=0, keepdims=True)

    m_prev = m_ref[...]
    row = lax.broadcasted_iota(jnp.int32, (chunk, chunk), 0)
    col = lax.broadcasted_iota(jnp.int32, (chunk, chunk), 1)
    dmat = jnp.where(col <= row, bcol + grow, -jnp.inf)
    inter = bcol + m_prev
    mj = jnp.maximum(inter, jnp.max(dmat, axis=-1, keepdims=True))
    w_inter = jnp.exp(inter - mj)

    q = q_ref[...]
    k = k_ref[...]
    v = v_ref[...]
    scale = M_DQK ** -0.5
    qk = lax.dot_general(q, k, (((1,), (1,)), ((), ())), preferred_element_type=F32) * scale
    p = jnp.exp(dmat - mj) * qk
    cq = jnp.dot(q, c_ref[...].astype(BF16), preferred_element_type=F32) * scale
    num = w_inter * cq + jnp.dot(p.astype(BF16), v, preferred_element_type=F32)
    qn = jnp.sum(q.astype(F32) * n_ref[...], axis=-1, keepdims=True) * scale
    nq = w_inter * qn + jnp.sum(p, axis=-1, keepdims=True)
    hv = num / jnp.maximum(jnp.abs(nq), jnp.exp(-mj))

    ms = jnp.mean(hv * hv, axis=-1, keepdims=True)
    hn = hv * lax.rsqrt(ms + EPS) * hn_ref[...]
    out_ref[...] = (hn * jax.nn.sigmoid(o_ref[...].astype(F32))).astype(out_ref.dtype)

    b_last = bcol[chunk - 1:chunk, :]
    acol = b_last + gcol
    m_new = jnp.maximum(b_last + m_prev, jnp.max(acol, axis=0, keepdims=True))
    decay = jnp.exp(b_last + m_prev - m_new)
    kw = k.astype(F32) * jnp.exp(acol - m_new)
    c_ref[...] = decay * c_ref[...] + lax.dot_general(
        kw.astype(BF16), v, (((0,), (0,)), ((), ())), preferred_element_type=F32)
    n_ref[...] = decay * n_ref[...] + jnp.sum(kw, axis=0, keepdims=True)
    m_ref[...] = m_new


def _mlstm(proj, bcol, gcol, grow, head_gain, *, batch, seq, chunk):
    m = proj.shape[0]
    nc = seq // chunk
    k_blk0 = (M_HEADS * M_DQK) // M_DQK
    v_blk0 = (2 * M_HEADS * M_DQK) // M_DV
    o_blk0 = v_blk0 + M_HEADS
    rows = lambda b, h, c: b * nc + c
    return pl.pallas_call(
        functools.partial(_mlstm_body, chunk=chunk),
        grid=(batch, M_HEADS, nc),
        in_specs=[
            pl.BlockSpec((chunk, M_DQK), lambda b, h, c: (rows(b, h, c), h)),
            pl.BlockSpec((chunk, M_DQK), lambda b, h, c: (rows(b, h, c), k_blk0 + h)),
            pl.BlockSpec((chunk, M_DV), lambda b, h, c: (rows(b, h, c), v_blk0 + h)),
            pl.BlockSpec((chunk, M_DV), lambda b, h, c: (rows(b, h, c), o_blk0 + h)),
            pl.BlockSpec((chunk, V7X_LANES), lambda b, h, c: (rows(b, h, c), 0)),
            pl.BlockSpec((chunk, V7X_LANES), lambda b, h, c: (rows(b, h, c), 0)),
            pl.BlockSpec((V7X_SUBLANES, chunk), lambda b, h, c: (0, rows(b, h, c))),
            pl.BlockSpec((1, M_DV), lambda b, h, c: (0, h)),
        ],
        out_specs=pl.BlockSpec((chunk, M_DV), lambda b, h, c: (rows(b, h, c), h)),
        out_shape=jax.ShapeDtypeStruct((m, M_HEADS * M_DV), BF16),
        scratch_shapes=[pltpu.VMEM((M_DQK, M_DV), F32),
                        pltpu.VMEM((1, M_DQK), F32),
                        pltpu.VMEM((1, 1), F32)],
        compiler_params=pltpu.CompilerParams(
            dimension_semantics=("parallel", "parallel", "arbitrary")),
        name="mlstm",
    )(proj, proj, proj, proj, bcol, gcol, grow, head_gain.reshape(1, M_HEADS * M_DV))


def _row_rmsnorm(a, gain):
    ms = jnp.mean(a * a, axis=-1, keepdims=True)
    return (a * lax.rsqrt(ms + EPS) * gain).astype(BF16)


def _qproj_body(a_ref, an_ref, w_ref, gq_ref, cos_ref, sin_lo_ref, sin_hi_ref, o_ref, *, heads_per_block):
    a = _row_rmsnorm(a_ref[...], an_ref[...])
    cos = cos_ref[...]
    sin = sin_lo_ref[...] + sin_hi_ref[...]
    gq = gq_ref[...]
    real = lax.broadcasted_iota(jnp.int32, (1, A_HEAD_PAD), 1) < A_QK
    acc = jnp.dot(a, w_ref[...], preferred_element_type=F32)
    for hh in range(heads_per_block):
        blk = acc[:, hh * A_HEAD_PAD:(hh + 1) * A_HEAD_PAD]
        ss = jnp.sum(jnp.where(real, blk * blk, 0.0), axis=-1, keepdims=True)
        qn = blk * lax.rsqrt(ss * (1.0 / A_QK) + EPS) * gq
        rope = qn[:, V7X_LANES:]
        rope = rope * cos + pltpu.roll(rope, shift=V7X_LANES - ROPE_HALF, axis=1) * sin
        o_ref[:, hh * A_HEAD_PAD:hh * A_HEAD_PAD + V7X_LANES] = qn[:, :V7X_LANES].astype(o_ref.dtype)
        o_ref[:, hh * A_HEAD_PAD + V7X_LANES:(hh + 1) * A_HEAD_PAD] = rope.astype(o_ref.dtype)


def _qproj(small, a_norm, w_q, gq, rope, *, rank, bm=512, heads_per_block=4):
    m = small.shape[0]
    bn = heads_per_block * A_HEAD_PAD
    n = w_q.shape[1]
    rowvec = pl.BlockSpec((bm, V7X_LANES), lambda i, j: (i, 0))
    return pl.pallas_call(
        functools.partial(_qproj_body, heads_per_block=heads_per_block),
        grid=(m // bm, n // bn),
        in_specs=[pl.BlockSpec((bm, rank), lambda i, j: (i, 0)),
                  pl.BlockSpec((1, rank), lambda i, j: (0, 0)),
                  pl.BlockSpec((rank, bn), lambda i, j: (0, j)),
                  pl.BlockSpec((1, A_HEAD_PAD), lambda i, j: (0, 0)),
                  rowvec, rowvec, rowvec],
        out_specs=pl.BlockSpec((bm, bn), lambda i, j: (i, j)),
        out_shape=jax.ShapeDtypeStruct((m, n), BF16),
        compiler_params=pltpu.CompilerParams(
            dimension_semantics=("parallel", "parallel"),
            vmem_limit_bytes=_vmem_limit([_nbytes((bm, rank), F32), _nbytes((rank, bn), BF16),
                                          _nbytes((bm, bn), BF16)],
                                         temp_bytes=4 * _nbytes((bm, bn), F32))),
        name="mla_qproj",
    )(small, a_norm.reshape(1, rank), w_q, gq, *rope)


def _kvproj_body(a_ref, an_ref, w_ref, kr_ref, gkn_ref, gkr_ref, cos_ref, sin_lo_ref, sin_hi_ref, k_ref, v_ref, *,
                 heads_per_block):
    a = _row_rmsnorm(a_ref[...], an_ref[...])
    kr = kr_ref[...]
    ss_rope = jnp.sum(kr * kr, axis=-1, keepdims=True)
    k_rot = _rotate(kr * gkr_ref[...], (cos_ref[...], sin_lo_ref[...], sin_hi_ref[...]))
    gkn = gkn_ref[...]
    ones = jnp.ones((v_ref.shape[0], A_DV), v_ref.dtype)
    acc = jnp.dot(a, w_ref[...], preferred_element_type=F32)
    for hh in range(heads_per_block):
        kv = acc[:, hh * (A_NOPE + A_DV):(hh + 1) * (A_NOPE + A_DV)]
        kn = kv[:, :A_NOPE]
        ss = jnp.sum(kn * kn, axis=-1, keepdims=True) + ss_rope
        rs = lax.rsqrt(ss * (1.0 / A_QK) + EPS)
        k_ref[:, hh * A_HEAD_PAD:hh * A_HEAD_PAD + V7X_LANES] = (kn * rs * gkn).astype(k_ref.dtype)
        k_ref[:, hh * A_HEAD_PAD + V7X_LANES:(hh + 1) * A_HEAD_PAD] = (k_rot * rs).astype(k_ref.dtype)
        v_ref[:, hh * A_V_PAD:hh * A_V_PAD + A_DV] = kv[:, A_NOPE:].astype(v_ref.dtype)
        v_ref[:, hh * A_V_PAD + A_DV:(hh + 1) * A_V_PAD] = ones


def _kvproj(small, a_norm, w_kv, gkn, gkr, rope, *, a_blk, rank, rope_blk, bm=512, heads_per_block=4):
    m = small.shape[0]
    bn = heads_per_block * (A_NOPE + A_DV)
    n_blocks = w_kv.shape[1] // bn
    rowvec = pl.BlockSpec((bm, V7X_LANES), lambda i, j: (i, 0))
    lanevec = pl.BlockSpec((1, V7X_LANES), lambda i, j: (0, 0))
    return pl.pallas_call(
        functools.partial(_kvproj_body, heads_per_block=heads_per_block),
        grid=(m // bm, n_blocks),
        in_specs=[pl.BlockSpec((bm, rank), lambda i, j: (i, a_blk)),
                  pl.BlockSpec((1, rank), lambda i, j: (0, 0)),
                  pl.BlockSpec((rank, bn), lambda i, j: (0, j)),
                  pl.BlockSpec((bm, V7X_LANES), lambda i, j: (i, rope_blk)),
                  lanevec, lanevec, rowvec, rowvec, rowvec],
        out_specs=[pl.BlockSpec((bm, heads_per_block * A_HEAD_PAD), lambda i, j: (i, j)),
                   pl.BlockSpec((bm, heads_per_block * A_V_PAD), lambda i, j: (i, j))],
        out_shape=[jax.ShapeDtypeStruct((m, A_HEADS * A_HEAD_PAD), BF16),
                   jax.ShapeDtypeStruct((m, A_HEADS * A_V_PAD), BF16)],
        compiler_params=pltpu.CompilerParams(
            dimension_semantics=("parallel", "parallel"),
            vmem_limit_bytes=_vmem_limit([_nbytes((bm, rank), F32), _nbytes((rank, bn), BF16),
                                          _nbytes((bm, bn), BF16), _nbytes((bm, bn), BF16)],
                                         temp_bytes=4 * _nbytes((bm, bn), F32))),
        name="mla_kvproj",
    )(small, a_norm.reshape(1, rank), w_kv, small, gkn, gkr, *rope)


FLASH_HEADS = 2


def _flash_body(q_ref, k_ref, v_ref, o_ref, s_ref, mx_ref, acc_ref, *, tq, tkc):
    i = pl.program_id(2)
    n_slab = tkc // V7X_LANES
    n_unmasked = (i * tq) // tkc
    n_chunks = ((i + 1) * tq + tkc - 1) // tkc
    mx_ref[...] = jnp.full_like(mx_ref, -jnp.inf)
    acc_ref[...] = jnp.zeros_like(acc_ref)

    def scores(c, masked):
        rows = pl.ds(pl.multiple_of(c * tkc, tkc), tkc)
        for e in range(FLASH_HEADS):
            s = lax.dot_general(q_ref[:, e * A_HEAD_PAD:(e + 1) * A_HEAD_PAD],
                                k_ref[rows, e * A_HEAD_PAD:(e + 1) * A_HEAD_PAD],
                                (((1,), (1,)), ((), ())), preferred_element_type=F32)
            if masked:
                row = lax.broadcasted_iota(jnp.int32, (tq, tkc), 0)
                col = lax.broadcasted_iota(jnp.int32, (tq, tkc), 1)
                s = jnp.where(col + c * tkc <= row + i * tq, s, -jnp.inf)
            s_ref[e, c] = s
            mx = mx_ref[e]
            for t in range(n_slab):
                mx = jnp.maximum(mx, s[:, t * V7X_LANES:(t + 1) * V7X_LANES])
            mx_ref[e] = mx

    def unmasked_chunk(c, carry):
        scores(c, False)
        return carry

    def masked_chunk(c, carry):
        scores(c, True)
        return carry

    lax.fori_loop(0, n_unmasked, unmasked_chunk, 0)
    lax.fori_loop(n_unmasked, n_chunks, masked_chunk, 0)

    for e in range(FLASH_HEADS):
        mx_ref[e] = jnp.broadcast_to(jnp.max(mx_ref[e], axis=-1, keepdims=True), (tq, V7X_LANES))

    def probs_chunk(c, carry):
        rows = pl.ds(pl.multiple_of(c * tkc, tkc), tkc)
        for e in range(FLASH_HEADS):
            s = s_ref[e, c]
            mx = mx_ref[e]
            p = jnp.concatenate([jnp.exp2(s[:, t * V7X_LANES:(t + 1) * V7X_LANES] - mx) for t in range(n_slab)],
                                axis=-1).astype(BF16)
            acc_ref[e] += jnp.dot(p, v_ref[rows, e * A_V_PAD:(e + 1) * A_V_PAD], preferred_element_type=F32)
        return carry

    lax.fori_loop(0, n_chunks, probs_chunk, 0)
    for e in range(FLASH_HEADS):
        acc = acc_ref[e]
        o_ref[:, e * A_DV:(e + 1) * A_DV] = (acc[:, :A_DV] / acc[:, A_DV:]).astype(o_ref.dtype)


def _flash_attention(q, k, v1, *, batch, seq, tq=512, tkc=1024):
    m = q.shape[0]
    nq = seq // tq
    hb = FLASH_HEADS
    blocks = [_nbytes((tq, hb * A_HEAD_PAD), BF16), _nbytes((seq, hb * A_HEAD_PAD), BF16),
              _nbytes((seq, hb * A_V_PAD), BF16), _nbytes((tq, hb * A_DV), BF16)]
    scratch_shapes = [pltpu.VMEM((hb, seq // tkc, tq, tkc), F32),
                      pltpu.VMEM((hb, tq, V7X_LANES), F32),
                      pltpu.VMEM((hb, tq, A_V_PAD), F32)]
    scratch = _nbytes((hb, tq, seq), F32) + _nbytes((hb, tq, V7X_LANES), F32) + _nbytes((hb, tq, A_V_PAD), F32)
    return pl.pallas_call(
        functools.partial(_flash_body, tq=tq, tkc=tkc),
        grid=(batch, A_HEADS // hb, nq),
        in_specs=[
            pl.BlockSpec((tq, hb * A_HEAD_PAD), lambda b, h, i: (b * nq + i, h)),
            pl.BlockSpec((seq, hb * A_HEAD_PAD), lambda b, h, i: (b, h)),
            pl.BlockSpec((seq, hb * A_V_PAD), lambda b, h, i: (b, h)),
        ],
        out_specs=pl.BlockSpec((tq, hb * A_DV), lambda b, h, i: (b * nq + i, h)),
        out_shape=jax.ShapeDtypeStruct((m, A_HEADS * A_DV), BF16),
        scratch_shapes=scratch_shapes,
        compiler_params=pltpu.CompilerParams(
            dimension_semantics=("parallel", "parallel", "parallel"),
            vmem_limit_bytes=_vmem_limit(blocks, scratch_bytes=scratch,
                                         temp_bytes=hb * 3 * _nbytes((tq, tkc), F32))),
        name="mla_flash",
    )(q, k, v1)


def _pad_lanes(x, width):
    return jnp.pad(x, [(0, 0)] * (x.ndim - 1) + [(0, width - x.shape[-1])])


def _layer(xf, rope, layer, norm_mix, w_in_stack, b_igate, b_fgate, m_head_norm, w_out_m_stack, q_a_norm,
           w_uq, kv_a_norm, w_ukv, qk_norm_q, qk_norm_k, w_out_a_stack, w_out_stack, norm_mlp, w_up_stack,
           w_down, *, batch, seq):
    d_model = xf.shape[1]
    q_lora = w_uq.shape[0]
    kv_lora = w_ukv.shape[0]
    m_qk_w = M_HEADS * M_DQK
    m_v_w = M_HEADS * M_DV
    w_in = w_in_stack[layer]
    o_i = 2 * m_qk_w + 2 * m_v_w
    o_f = o_i + M_HEADS
    o_cq = o_f + M_HEADS
    o_kva = o_cq + q_lora
    o_ga = o_kva + kv_lora + A_ROPE
    w_main = jnp.concatenate([w_in[:, :o_i], w_in[:, o_ga:]], axis=1).astype(BF16)
    w_small = jnp.concatenate([
        _pad_lanes(w_in[:, o_cq:o_ga], q_lora + kv_lora + V7X_LANES),
        _pad_lanes(w_in[:, o_i:o_f], V7X_LANES),
        _pad_lanes(w_in[:, o_f:o_cq], V7X_LANES),
    ], axis=1).astype(BF16)
    kv_blk = q_lora // kv_lora
    rope_blk = (q_lora + kv_lora) // V7X_LANES
    i_blk = rope_blk + 1
    f_blk = i_blk + 1

    twice_rope = lambda t: jnp.concatenate([t, t[..., A_NOPE:]], axis=-1)
    w_q = twice_rope(w_uq.astype(BF16).reshape(q_lora, A_HEADS, A_QK)).reshape(q_lora, A_HEADS * A_HEAD_PAD)
    gq = twice_rope(qk_norm_q * (A_QK ** -0.5 * LOG2E)).reshape(1, A_HEAD_PAD)
    hpb = 4
    w_kv = w_ukv.astype(BF16)
    gkn = qk_norm_k[:A_NOPE].reshape(1, V7X_LANES)
    gkr = _pad_lanes(qk_norm_k[A_NOPE:], V7X_LANES).reshape(1, V7X_LANES)

    h = _rmsnorm(xf, norm_mix)
    proj = _matmul(h, w_main, bm=1024, bn=1024, out_dtype=BF16, name="in_proj_main")
    small = _matmul(h, w_small, bm=1024, bn=w_small.shape[1] // 3, out_dtype=F32, name="in_proj_small")

    bcol, gcol, grow = _mlstm_gates(small, b_igate, b_fgate, i_blk=i_blk, f_blk=f_blk, chunk=MLSTM_CHUNK)
    hm = _mlstm(proj, bcol, gcol, grow, m_head_norm, batch=batch, seq=seq, chunk=MLSTM_CHUNK)

    qh = _qproj(small, q_a_norm, w_q, gq, rope, rank=q_lora, heads_per_block=hpb)
    kh, vh = _kvproj(small, kv_a_norm, w_kv, gkn, gkr, rope, a_blk=kv_blk, rank=kv_lora,
                     rope_blk=rope_blk, heads_per_block=hpb)
    att = _flash_attention(qh, kh, vh, batch=batch, seq=seq)

    bn = 512
    ga_blk0 = o_i // bn
    gb_blk0 = ga_blk0 + d_model // bn
    t = _matmul_wres(hm, w_out_m_stack, layer, n_cols=d_model, bm=1024, bn=bn, out_dtype=F32,
                     epilogue=_ep_gate, extras=[(proj, ga_blk0)], name="out_proj_mlstm")
    merged = _matmul_wres(att, w_out_a_stack, layer, n_cols=d_model, bm=1024, bn=bn, out_dtype=BF16,
                          epilogue=_ep_gate_add, extras=[(proj, gb_blk0), (t, 0)], name="out_proj_mla")
    xf = _matmul_wres(merged, w_out_stack, layer, n_cols=d_model, bm=1024, bn=bn, out_dtype=F32,
                      epilogue=_ep_residual, extras=[(xf, 0)], name="out_proj")

    h2 = _rmsnorm(xf, norm_mlp)
    u = _matmul_wres(h2, w_up_stack, layer, n_cols=w_up_stack.shape[2], bm=1024, bn=bn, out_dtype=BF16,
                     epilogue=_ep_relu2, name="mlp_up")
    return _matmul_kacc(u, w_down.astype(BF16), xf, bm=1024, bn=1024, bk=4096, name="mlp_down")


def kernel(x, positions, norm_mix, w_in, b_igate, b_fgate, m_head_norm, w_out_m, q_a_norm, w_uq,
           kv_a_norm, w_ukv, qk_norm_q, qk_norm_k, w_out_a, w_out, norm_mlp, w_up, w_down):
    batch, seq, d_model = x.shape
    depth = w_in.shape[0]
    m = batch * seq
    inv_freq = ROPE_THETA ** (-jnp.arange(0, A_ROPE, 2, dtype=F32) / A_ROPE)
    ang = positions.astype(F32).reshape(m, 1) * inv_freq
    rope = _rope_tables(_pad_lanes(jnp.concatenate([ang, ang], axis=-1), V7X_LANES))
    xf = x.reshape(m, d_model)
    for l in range(depth):
        xf = _layer(xf, rope, l, norm_mix[l], w_in, b_igate[l], b_fgate[l], m_head_norm[l], w_out_m,
                    q_a_norm[l], w_uq[l], kv_a_norm[l], w_ukv[l], qk_norm_q[l], qk_norm_k[l], w_out_a,
                    w_out, norm_mlp[l], w_up, w_down[l], batch=batch, seq=seq)
    return xf.reshape(batch, seq, d_model)
```

```python
import functools
import math

import jax
import jax.numpy as jnp
from jax import lax
from jax.experimental import pallas as pl
from jax.experimental.pallas import tpu as pltpu

F32 = jnp.float32
BF16 = jnp.bfloat16

M_HEADS = 8
M_DQK = 256
M_DV = 512
GATE_SOFTCAP = 15.0
A_HEADS = 32
A_NOPE = 128
A_ROPE = 64
A_QK = A_NOPE + A_ROPE
A_DV = 128
ROPE_THETA = 10000.0
EPS = 1e-6

V7X_LANES = 128
V7X_SUBLANES = 8
V7X_VMEM_BYTES = 64 * 1024 * 1024
V7X_VMEM_REQUEST_CAP = V7X_VMEM_BYTES - 6 * 1024 * 1024

A_HEAD_PAD = 2 * V7X_LANES
A_V_PAD = 2 * A_DV
ROPE_HALF = A_ROPE // 2
MLSTM_CHUNK = 256
LOG2E = math.log2(math.e)


def _vmem_limit(block_bytes, scratch_bytes=0, temp_bytes=0):
    need = 2 * sum(block_bytes) + scratch_bytes + temp_bytes + (2 << 20)
    return int(min(max(need, 16 << 20), V7X_VMEM_REQUEST_CAP))


def _nbytes(shape, dtype):
    return math.prod(shape) * jnp.dtype(dtype).itemsize


def _rmsnorm_body(x_ref, g_ref, o_ref):
    x = x_ref[...]
    ms = jnp.mean(x * x, axis=-1, keepdims=True)
    o_ref[...] = (x * lax.rsqrt(ms + EPS) * g_ref[...]).astype(o_ref.dtype)


def _rmsnorm(x, g, *, tm=256):
    m, d = x.shape
    return pl.pallas_call(
        _rmsnorm_body,
        grid=(m // tm,),
        in_specs=[pl.BlockSpec((tm, d), lambda i: (i, 0)),
                  pl.BlockSpec((1, d), lambda i: (0, 0))],
        out_specs=pl.BlockSpec((tm, d), lambda i: (i, 0)),
        out_shape=jax.ShapeDtypeStruct((m, d), BF16),
        compiler_params=pltpu.CompilerParams(
            dimension_semantics=("parallel",),
            vmem_limit_bytes=_vmem_limit([_nbytes((tm, d), F32), _nbytes((tm, d), BF16)],
                                         temp_bytes=2 * _nbytes((tm, d), F32))),
        name="rmsnorm",
    )(x, g.reshape(1, d))


def _ep_identity(acc):
    return acc


def _ep_relu2(acc):
    return jnp.square(jnp.maximum(acc, 0.0))


def _ep_gate(acc, g):
    return jax.nn.sigmoid(g.astype(F32)) * acc


def _ep_gate_add(acc, g, t):
    return t + jax.nn.sigmoid(g.astype(F32)) * acc


def _ep_residual(acc, r):
    return r + acc


def _mm_body(*refs, n_extra, epilogue):
    a_ref, w_ref = refs[0], refs[1]
    extra_refs = refs[2:2 + n_extra]
    o_ref = refs[2 + n_extra]
    acc = jnp.dot(a_ref[...], w_ref[...], preferred_element_type=F32)
    o_ref[...] = epilogue(acc, *[r[...] for r in extra_refs]).astype(o_ref.dtype)


def _matmul(a, w, *, bm, bn, out_dtype, epilogue=_ep_identity, extras=(), layer=None, n_cols=None, name):
    m, kdim = a.shape
    if layer is None:
        n = w.shape[1]
        w_spec = pl.BlockSpec((kdim, bn), lambda i, j: (0, j))
    else:
        n = n_cols
        w_spec = pl.BlockSpec((None, kdim, bn), lambda i, j: (layer, 0, j))
    in_specs = [pl.BlockSpec((bm, kdim), lambda i, j: (i, 0)), w_spec]
    blocks = [_nbytes((bm, kdim), a.dtype), _nbytes((kdim, bn), w.dtype), _nbytes((bm, bn), out_dtype)]
    for arr, off in extras:
        in_specs.append(pl.BlockSpec((bm, bn), lambda i, j, off=off: (i, off + j)))
        blocks.append(_nbytes((bm, bn), arr.dtype))
    return pl.pallas_call(
        functools.partial(_mm_body, n_extra=len(extras), epilogue=epilogue),
        grid=(m // bm, n // bn),
        in_specs=in_specs,
        out_specs=pl.BlockSpec((bm, bn), lambda i, j: (i, j)),
        out_shape=jax.ShapeDtypeStruct((m, n), out_dtype),
        compiler_params=pltpu.CompilerParams(
            dimension_semantics=("parallel", "parallel"),
            vmem_limit_bytes=_vmem_limit(blocks, temp_bytes=3 * _nbytes((bm, bn), F32))),
        name=name,
    )(a, w, *[arr for arr, _ in extras])


def _mm_kacc_body(a_ref, w_ref, r_ref, o_ref):
    part = jnp.dot(a_ref[...], w_ref[...], preferred_element_type=F32)

    @pl.when(pl.program_id(2) == 0)
    def _():
        o_ref[...] = r_ref[...] + part

    @pl.when(pl.program_id(2) > 0)
    def _():
        o_ref[...] += part


def _matmul_kacc(a, w_stack, layer, res, *, bm, bn, bk, name):
    m, kdim = a.shape
    n = w_stack.shape[2]
    blocks = [_nbytes((bm, bk), a.dtype), _nbytes((bk, bn), w_stack.dtype), 2 * _nbytes((bm, bn), F32)]
    return pl.pallas_call(
        _mm_kacc_body,
        grid=(m // bm, n // bn, kdim // bk),
        in_specs=[pl.BlockSpec((bm, bk), lambda i, j, k: (i, k)),
                  pl.BlockSpec((None, bk, bn), lambda i, j, k: (layer, k, j)),
                  pl.BlockSpec((bm, bn), lambda i, j, k: (i, j))],
        out_specs=pl.BlockSpec((bm, bn), lambda i, j, k: (i, j)),
        out_shape=jax.ShapeDtypeStruct((m, n), F32),
        compiler_params=pltpu.CompilerParams(
            dimension_semantics=("parallel", "parallel", "arbitrary"),
            vmem_limit_bytes=_vmem_limit(blocks, temp_bytes=2 * _nbytes((bm, bn), F32))),
        name=name,
    )(a, w_stack, res)


def _mm_wstat_body(*refs, n_extra, epilogue, n_col_blocks):
    a_ref, wn_ref = refs[0], refs[1]
    extra_refs = refs[2:2 + n_extra]
    o_ref = refs[2 + n_extra]
    wb_ref = refs[3 + n_extra]
    p = pl.program_id(0)
    i = pl.program_id(1)
    fill = p % 2
    kc = wn_ref.shape[0]

    @pl.when(p < n_col_blocks)
    def _():
        wb_ref[fill, pl.ds(pl.multiple_of(i * kc, kc), kc), :] = wn_ref[...].astype(BF16)

    @pl.when(p > 0)
    def _():
        acc = jnp.dot(a_ref[...], wb_ref[1 - fill], preferred_element_type=F32)
        o_ref[...] = epilogue(acc, *[r[...] for r in extra_refs]).astype(o_ref.dtype)


def _matmul_wstat(a, w_stack, layer, *, bm, bn, out_dtype, epilogue=_ep_identity, extras=(), name):
    m, kdim = a.shape
    n = w_stack.shape[2]
    n_row_blocks = m // bm
    n_col_blocks = n // bn
    kc = kdim // n_row_blocks
    row = lambda p, i: jnp.where(p == 0, 0, i)
    col = lambda p: jnp.maximum(p - 1, 0)
    in_specs = [pl.BlockSpec((bm, kdim), lambda p, i: (row(p, i), 0)),
                pl.BlockSpec((None, kc, bn), lambda p, i: (layer, i, jnp.minimum(p, n_col_blocks - 1)))]
    blocks = [_nbytes((bm, kdim), a.dtype), _nbytes((kc, bn), F32), _nbytes((bm, bn), out_dtype)]
    for arr, off in extras:
        in_specs.append(pl.BlockSpec((bm, bn), lambda p, i, off=off: (row(p, i), off + col(p))))
        blocks.append(_nbytes((bm, bn), arr.dtype))
    return pl.pallas_call(
        functools.partial(_mm_wstat_body, n_extra=len(extras), epilogue=epilogue, n_col_blocks=n_col_blocks),
        grid=(n_col_blocks + 1, n_row_blocks),
        in_specs=in_specs,
        out_specs=pl.BlockSpec((bm, bn), lambda p, i: (row(p, i), col(p))),
        out_shape=jax.ShapeDtypeStruct((m, n), out_dtype),
        scratch_shapes=[pltpu.VMEM((2, kdim, bn), BF16)],
        compiler_params=pltpu.CompilerParams(
            dimension_semantics=("arbitrary", "arbitrary"),
            vmem_limit_bytes=_vmem_limit(blocks, scratch_bytes=2 * _nbytes((kdim, bn), BF16),
                                         temp_bytes=3 * _nbytes((bm, bn), F32))),
        name=name,
    )(a, w_stack, *[arr for arr, _ in extras])


def _mm_wres_body(*refs, n_extra, epilogue, k_chunk):
    a_ref, w_ref = refs[0], refs[1]
    extra_refs = refs[2:2 + n_extra]
    o_ref = refs[2 + n_extra]
    wb_ref = refs[3 + n_extra]

    @pl.when(pl.program_id(1) == 0)
    def _():
        for c in range(wb_ref.shape[0] // k_chunk):
            rows = slice(c * k_chunk, (c + 1) * k_chunk)
            wb_ref[rows, :] = w_ref[rows, :].astype(BF16)

    acc = jnp.dot(a_ref[...], wb_ref[...], preferred_element_type=F32)
    o_ref[...] = epilogue(acc, *[r[...] for r in extra_refs]).astype(o_ref.dtype)


def _matmul_wres(a, w_stack, layer, *, n_cols, bm, bn, out_dtype, epilogue=_ep_identity, extras=(), name):
    m, kdim = a.shape
    in_specs = [pl.BlockSpec((bm, kdim), lambda j, i: (i, 0)),
                pl.BlockSpec((None, kdim, bn), lambda j, i: (layer, 0, j))]
    blocks = [_nbytes((bm, kdim), a.dtype), _nbytes((kdim, bn), F32), _nbytes((bm, bn), out_dtype)]
    for arr, off in extras:
        in_specs.append(pl.BlockSpec((bm, bn), lambda j, i, off=off: (i, off + j)))
        blocks.append(_nbytes((bm, bn), arr.dtype))
    return pl.pallas_call(
        functools.partial(_mm_wres_body, n_extra=len(extras), epilogue=epilogue, k_chunk=512),
        grid=(n_cols // bn, m // bm),
        in_specs=in_specs,
        out_specs=pl.BlockSpec((bm, bn), lambda j, i: (i, j)),
        out_shape=jax.ShapeDtypeStruct((m, n_cols), out_dtype),
        scratch_shapes=[pltpu.VMEM((kdim, bn), BF16)],
        compiler_params=pltpu.CompilerParams(
            dimension_semantics=("parallel", "arbitrary"),
            vmem_limit_bytes=_vmem_limit(blocks, scratch_bytes=_nbytes((kdim, bn), BF16),
                                         temp_bytes=3 * _nbytes((bm, bn), F32))),
        name=name,
    )(a, w_stack, *[arr for arr, _ in extras])


def _rope_body(ang_ref, cmask_ref, lo_sign_ref, hi_sign_ref, cos_ref, sin_lo_ref, sin_hi_ref):
    ang = ang_ref[...]
    sin = jnp.sin(ang)
    cos_ref[...] = jnp.cos(ang) * cmask_ref[...]
    sin_lo_ref[...] = sin * lo_sign_ref[...]
    sin_hi_ref[...] = sin * hi_sign_ref[...]


def _rope_tables(ang128, *, tm=1024):
    m = ang128.shape[0]
    lane = jnp.arange(V7X_LANES)
    lo = (lane < ROPE_HALF).astype(F32).reshape(1, V7X_LANES)
    hi = jnp.logical_and(lane >= ROPE_HALF, lane < A_ROPE).astype(F32).reshape(1, V7X_LANES)
    row = pl.BlockSpec((tm, V7X_LANES), lambda i: (i, 0))
    vec = pl.BlockSpec((1, V7X_LANES), lambda i: (0, 0))
    return pl.pallas_call(
        _rope_body,
        grid=(m // tm,),
        in_specs=[row, vec, vec, vec],
        out_specs=[row, row, row],
        out_shape=[jax.ShapeDtypeStruct((m, V7X_LANES), F32)] * 3,
        compiler_params=pltpu.CompilerParams(dimension_semantics=("parallel",)),
        name="rope_tables",
    )(ang128, lo + hi, -lo, hi)


def _rotate(x, rope):
    cos, sin_lo, sin_hi = rope
    hi_on_lo = pltpu.roll(x, shift=V7X_LANES - ROPE_HALF, axis=1)
    lo_on_hi = pltpu.roll(x, shift=ROPE_HALF, axis=1)
    return x * cos + hi_on_lo * sin_lo + lo_on_hi * sin_hi


def _split_bf16(x):
    hi = x.astype(BF16)
    r1 = x - hi.astype(F32)
    mid = r1.astype(BF16)
    lo = (r1 - mid.astype(F32)).astype(BF16)
    return hi, mid, lo


def _gate_body(i_ref, f_ref, bi_ref, bf_ref, bcol_ref, gcol_ref, grow_ref, *, chunk):
    li = GATE_SOFTCAP * jnp.tanh((i_ref[...] + bi_ref[...]) / GATE_SOFTCAP)
    z = GATE_SOFTCAP * jnp.tanh((f_ref[...] + bf_ref[...]) / GATE_SOFTCAP)
    lf = jnp.minimum(z, 0.0) - jnp.log(1.0 + jnp.exp(-jnp.abs(z)))
    row = lax.broadcasted_iota(jnp.int32, (chunk, chunk), 0)
    col = lax.broadcasted_iota(jnp.int32, (chunk, chunk), 1)
    tril = (col <= row).astype(BF16)
    b = sum(jnp.dot(tril, piece, preferred_element_type=F32) for piece in _split_bf16(lf))
    g = li - b
    bcol_ref[...] = b
    gcol_ref[...] = g
    grow_ref[...] = g.T[:V7X_SUBLANES, :]


def _mlstm_gates(small, b_i, b_f, *, i_blk, f_blk, chunk):
    m = small.shape[0]
    pad = lambda b: jnp.pad(b.astype(F32), (0, V7X_LANES - b.shape[0])).reshape(1, V7X_LANES)
    col = pl.BlockSpec((chunk, V7X_LANES), lambda c: (c, 0))
    vec = pl.BlockSpec((1, V7X_LANES), lambda c: (0, 0))
    return pl.pallas_call(
        functools.partial(_gate_body, chunk=chunk),
        grid=(m // chunk,),
        in_specs=[pl.BlockSpec((chunk, V7X_LANES), lambda c: (c, i_blk)),
                  pl.BlockSpec((chunk, V7X_LANES), lambda c: (c, f_blk)),
                  vec, vec],
        out_specs=[col, col, pl.BlockSpec((V7X_SUBLANES, chunk), lambda c: (0, c))],
        out_shape=[jax.ShapeDtypeStruct((m, V7X_LANES), F32),
                   jax.ShapeDtypeStruct((m, V7X_LANES), F32),
                   jax.ShapeDtypeStruct((V7X_SUBLANES, m), F32)],
        compiler_params=pltpu.CompilerParams(dimension_semantics=("parallel",)),
        name="mlstm_gates",
    )(small, small, pad(b_i), pad(b_f))


def _mlstm_body(q_ref, k_ref, v_ref, o_ref, bcol_ref, gcol_ref, grow_ref, hn_ref, out_ref,
                c_ref, n_ref, m_ref, *, chunk):
    head = pl.program_id(1)

    @pl.when(pl.program_id(2) == 0)
    def _():
        c_ref[...] = jnp.zeros_like(c_ref)
        n_ref[...] = jnp.zeros_like(n_ref)
        m_ref[...] = jnp.zeros_like(m_ref)

    lane_sel = lax.broadcasted_iota(jnp.int32, (1, V7X_LANES), 1) == head
    bcol = jnp.sum(jnp.where(lane_sel, bcol_ref[...], 0.0), axis=-1, keepdims=True)
    gcol = jnp.sum(jnp.where(lane_sel, gcol_ref[...], 0.0), axis=-1, keepdims=True)
    sub_sel = lax.broadcasted_iota(jnp.int32, (V7X_SUBLANES, 1), 0) == head
    grow = jnp.sum(jnp.where(sub_sel, grow_ref[...], 0.0), axis=0, keepdims=True)

    m_prev = m_ref[...]
    row = lax.broadcasted_iota(jnp.int32, (chunk, chunk), 0)
    col = lax.broadcasted_iota(jnp.int32, (chunk, chunk), 1)
    dmat = jnp.where(col <= row, bcol + grow, -jnp.inf)
    inter = bcol + m_prev
    mj = jnp.maximum(inter, jnp.max(dmat, axis=-1, keepdims=True))
    w_inter = jnp.exp(inter - mj)

    q = q_ref[...]
    k = k_ref[...]
    v = v_ref[...]
    scale = M_DQK ** -0.5
    qk = lax.dot_general(q, k, (((1,), (1,)), ((), ())), preferred_element_type=F32) * scale
    p = jnp.exp(dmat - mj) * qk
    cq = jnp.dot(q, c_ref[...].astype(BF16), preferred_element_type=F32) * scale
    num = w_inter * cq + jnp.dot(p.astype(BF16), v, preferred_element_type=F32)
    qn = jnp.sum(q.astype(F32) * n_ref[...], axis=-1, keepdims=True) * scale
    nq = w_inter * qn + jnp.sum(p, axis=-1, keepdims=True)
    hv = num / jnp.maximum(jnp.abs(nq), jnp.exp(-mj))

    ms = jnp.mean(hv * hv, axis=-1, keepdims=True)
    hn = hv * lax.rsqrt(ms + EPS) * hn_ref[...]
    out_ref[...] = (hn * jax.nn.sigmoid(o_ref[...].astype(F32))).astype(out_ref.dtype)

    b_last = bcol[chunk - 1:chunk, :]
    acol = b_last + gcol
    m_new = jnp.maximum(b_last + m_prev, jnp.max(acol, axis=0, keepdims=True))
    decay = jnp.exp(b_last + m_prev - m_new)
    kw = k.astype(F32) * jnp.exp(acol - m_new)
    c_ref[...] = decay * c_ref[...] + lax.dot_general(
        kw.astype(BF16), v, (((0,), (0,)), ((), ())), preferred_element_type=F32)
    n_ref[...] = decay * n_ref[...] + jnp.sum(kw, axis=0, keepdims=True)
    m_ref[...] = m_new


def _mlstm(proj, bcol, gcol, grow, head_gain, *, batch, seq, chunk):
    m = proj.shape[0]
    nc = seq // chunk
    k_blk0 = (M_HEADS * M_DQK) // M_DQK
    v_blk0 = (2 * M_HEADS * M_DQK) // M_DV
    o_blk0 = v_blk0 + M_HEADS
    rows = lambda b, h, c: b * nc + c
    return pl.pallas_call(
        functools.partial(_mlstm_body, chunk=chunk),
        grid=(batch, M_HEADS, nc),
        in_specs=[
            pl.BlockSpec((chunk, M_DQK), lambda b, h, c: (rows(b, h, c), h)),
            pl.BlockSpec((chunk, M_DQK), lambda b, h, c: (rows(b, h, c), k_blk0 + h)),
            pl.BlockSpec((chunk, M_DV), lambda b, h, c: (rows(b, h, c), v_blk0 + h)),
            pl.BlockSpec((chunk, M_DV), lambda b, h, c: (rows(b, h, c), o_blk0 + h)),
            pl.BlockSpec((chunk, V7X_LANES), lambda b, h, c: (rows(b, h, c), 0)),
            pl.BlockSpec((chunk, V7X_LANES), lambda b, h, c: (rows(b, h, c), 0)),
            pl.BlockSpec((V7X_SUBLANES, chunk), lambda b, h, c: (0, rows(b, h, c))),
            pl.BlockSpec((1, M_DV), lambda b, h, c: (0, h)),
        ],
        out_specs=pl.BlockSpec((chunk, M_DV), lambda b, h, c: (rows(b, h, c), h)),
        out_shape=jax.ShapeDtypeStruct((m, M_HEADS * M_DV), BF16),
        scratch_shapes=[pltpu.VMEM((M_DQK, M_DV), F32),
                        pltpu.VMEM((1, M_DQK), F32),
                        pltpu.VMEM((1, 1), F32)],
        compiler_params=pltpu.CompilerParams(
            dimension_semantics=("parallel", "parallel", "arbitrary")),
        name="mlstm",
    )(proj, proj, proj, proj, bcol, gcol, grow, head_gain.reshape(1, M_HEADS * M_DV))


def _row_rmsnorm(a, gain):
    ms = jnp.mean(a * a, axis=-1, keepdims=True)
    return (a * lax.rsqrt(ms + EPS) * gain).astype(BF16)


def _qproj_body(a_ref, an_ref, w_ref, gq_ref, cos_ref, sin_lo_ref, sin_hi_ref, o_ref, *, heads_per_block):
    a = _row_rmsnorm(a_ref[...], an_ref[...])
    cos = cos_ref[...]
    sin = sin_lo_ref[...] + sin_hi_ref[...]
    gq = gq_ref[...]
    real = lax.broadcasted_iota(jnp.int32, (1, A_HEAD_PAD), 1) < A_QK
    acc = jnp.dot(a, w_ref[...], preferred_element_type=F32)
    for hh in range(heads_per_block):
        blk = acc[:, hh * A_HEAD_PAD:(hh + 1) * A_HEAD_PAD]
        ss = jnp.sum(jnp.where(real, blk * blk, 0.0), axis=-1, keepdims=True)
        qn = blk * lax.rsqrt(ss * (1.0 / A_QK) + EPS) * gq
        rope = qn[:, V7X_LANES:]
        rope = rope * cos + pltpu.roll(rope, shift=V7X_LANES - ROPE_HALF, axis=1) * sin
        o_ref[:, hh * A_HEAD_PAD:hh * A_HEAD_PAD + V7X_LANES] = qn[:, :V7X_LANES].astype(o_ref.dtype)
        o_ref[:, hh * A_HEAD_PAD + V7X_LANES:(hh + 1) * A_HEAD_PAD] = rope.astype(o_ref.dtype)


def _qproj(small, a_norm, w_q, gq, rope, *, rank, bm=512, heads_per_block=4):
    m = small.shape[0]
    bn = heads_per_block * A_HEAD_PAD
    n = w_q.shape[1]
    rowvec = pl.BlockSpec((bm, V7X_LANES), lambda i, j: (i, 0))
    return pl.pallas_call(
        functools.partial(_qproj_body, heads_per_block=heads_per_block),
        grid=(m // bm, n // bn),
        in_specs=[pl.BlockSpec((bm, rank), lambda i, j: (i, 0)),
                  pl.BlockSpec((1, rank), lambda i, j: (0, 0)),
                  pl.BlockSpec((rank, bn), lambda i, j: (0, j)),
                  pl.BlockSpec((1, A_HEAD_PAD), lambda i, j: (0, 0)),
                  rowvec, rowvec, rowvec],
        out_specs=pl.BlockSpec((bm, bn), lambda i, j: (i, j)),
        out_shape=jax.ShapeDtypeStruct((m, n), BF16),
        compiler_params=pltpu.CompilerParams(
            dimension_semantics=("parallel", "parallel"),
            vmem_limit_bytes=_vmem_limit([_nbytes((bm, rank), F32), _nbytes((rank, bn), BF16),
                                          _nbytes((bm, bn), BF16)],
                                         temp_bytes=4 * _nbytes((bm, bn), F32))),
        name="mla_qproj",
    )(small, a_norm.reshape(1, rank), w_q, gq, *rope)


def _kvproj_body(a_ref, an_ref, w_ref, kr_ref, gkn_ref, gkr_ref, cos_ref, sin_lo_ref, sin_hi_ref, k_ref, v_ref, *,
                 heads_per_block):
    a = _row_rmsnorm(a_ref[...], an_ref[...])
    kr = kr_ref[...]
    ss_rope = jnp.sum(kr * kr, axis=-1, keepdims=True)
    k_rot = _rotate(kr * gkr_ref[...], (cos_ref[...], sin_lo_ref[...], sin_hi_ref[...]))
    gkn = gkn_ref[...]
    ones = jnp.ones((v_ref.shape[0], A_DV), v_ref.dtype)
    acc = jnp.dot(a, w_ref[...], preferred_element_type=F32)
    for hh in range(heads_per_block):
        kv = acc[:, hh * (A_NOPE + A_DV):(hh + 1) * (A_NOPE + A_DV)]
        kn = kv[:, :A_NOPE]
        ss = jnp.sum(kn * kn, axis=-1, keepdims=True) + ss_rope
        rs = lax.rsqrt(ss * (1.0 / A_QK) + EPS)
        k_ref[:, hh * A_HEAD_PAD:hh * A_HEAD_PAD + V7X_LANES] = (kn * rs * gkn).astype(k_ref.dtype)
        k_ref[:, hh * A_HEAD_PAD + V7X_LANES:(hh + 1) * A_HEAD_PAD] = (k_rot * rs).astype(k_ref.dtype)
        v_ref[:, hh * A_V_PAD:hh * A_V_PAD + A_DV] = kv[:, A_NOPE:].astype(v_ref.dtype)
        v_ref[:, hh * A_V_PAD + A_DV:(hh + 1) * A_V_PAD] = ones


def _kvproj(small, a_norm, w_kv, gkn, gkr, rope, *, a_blk, rank, rope_blk, bm=512, heads_per_block=4):
    m = small.shape[0]
    bn = heads_per_block * (A_NOPE + A_DV)
    n_blocks = w_kv.shape[1] // bn
    rowvec = pl.BlockSpec((bm, V7X_LANES), lambda i, j: (i, 0))
    lanevec = pl.BlockSpec((1, V7X_LANES), lambda i, j: (0, 0))
    return pl.pallas_call(
        functools.partial(_kvproj_body, heads_per_block=heads_per_block),
        grid=(m // bm, n_blocks),
        in_specs=[pl.BlockSpec((bm, rank), lambda i, j: (i, a_blk)),
                  pl.BlockSpec((1, rank), lambda i, j: (0, 0)),
                  pl.BlockSpec((rank, bn), lambda i, j: (0, j)),
                  pl.BlockSpec((bm, V7X_LANES), lambda i, j: (i, rope_blk)),
                  lanevec, lanevec, rowvec, rowvec, rowvec],
        out_specs=[pl.BlockSpec((bm, heads_per_block * A_HEAD_PAD), lambda i, j: (i, j)),
                   pl.BlockSpec((bm, heads_per_block * A_V_PAD), lambda i, j: (i, j))],
        out_shape=[jax.ShapeDtypeStruct((m, A_HEADS * A_HEAD_PAD), BF16),
                   jax.ShapeDtypeStruct((m, A_HEADS * A_V_PAD), BF16)],
        compiler_params=pltpu.CompilerParams(
            dimension_semantics=("parallel", "parallel"),
            vmem_limit_bytes=_vmem_limit([_nbytes((bm, rank), F32), _nbytes((rank, bn), BF16),
                                          _nbytes((bm, bn), BF16), _nbytes((bm, bn), BF16)],
                                         temp_bytes=4 * _nbytes((bm, bn), F32))),
        name="mla_kvproj",
    )(small, a_norm.reshape(1, rank), w_kv, small, gkn, gkr, *rope)


FLASH_HEADS = 2


def _flash_body(q_ref, k_ref, v_ref, o_ref, s_ref, mx_ref, acc_ref, *, tq, tkc):
    i = pl.program_id(2)
    n_slab = tkc // V7X_LANES
    n_unmasked = (i * tq) // tkc
    n_chunks = ((i + 1) * tq + tkc - 1) // tkc
    mx_ref[...] = jnp.full_like(mx_ref, -jnp.inf)
    acc_ref[...] = jnp.zeros_like(acc_ref)

    def scores(c, masked):
        rows = pl.ds(pl.multiple_of(c * tkc, tkc), tkc)
        for e in range(FLASH_HEADS):
            s = lax.dot_general(q_ref[:, e * A_HEAD_PAD:(e + 1) * A_HEAD_PAD],
                                k_ref[rows, e * A_HEAD_PAD:(e + 1) * A_HEAD_PAD],
                                (((1,), (1,)), ((), ())), preferred_element_type=F32)
            if masked:
                row = lax.broadcasted_iota(jnp.int32, (tq, tkc), 0)
                col = lax.broadcasted_iota(jnp.int32, (tq, tkc), 1)
                s = jnp.where(col + c * tkc <= row + i * tq, s, -jnp.inf)
            s_ref[e, c] = s
            mx = mx_ref[e]
            for t in range(n_slab):
                mx = jnp.maximum(mx, s[:, t * V7X_LANES:(t + 1) * V7X_LANES])
            mx_ref[e] = mx

    def unmasked_chunk(c, carry):
        scores(c, False)
        return carry

    def masked_chunk(c, carry):
        scores(c, True)
        return carry

    lax.fori_loop(0, n_unmasked, unmasked_chunk, 0)
    lax.fori_loop(n_unmasked, n_chunks, masked_chunk, 0)

    for e in range(FLASH_HEADS):
        mx_ref[e] = jnp.broadcast_to(jnp.max(mx_ref[e], axis=-1, keepdims=True), (tq, V7X_LANES))

    def probs_chunk(c, carry):
        rows = pl.ds(pl.multiple_of(c * tkc, tkc), tkc)
        for e in range(FLASH_HEADS):
            s = s_ref[e, c]
            mx = mx_ref[e]
            p = jnp.concatenate([jnp.exp2(s[:, t * V7X_LANES:(t + 1) * V7X_LANES] - mx) for t in range(n_slab)],
                                axis=-1).astype(BF16)
            acc_ref[e] += jnp.dot(p, v_ref[rows, e * A_V_PAD:(e + 1) * A_V_PAD], preferred_element_type=F32)
        return carry

    lax.fori_loop(0, n_chunks, probs_chunk, 0)
    for e in range(FLASH_HEADS):
        acc = acc_ref[e]
        o_ref[:, e * A_DV:(e + 1) * A_DV] = (acc[:, :A_DV] / acc[:, A_DV:]).astype(o_ref.dtype)


def _flash_attention(q, k, v1, *, batch, seq, tq=512, tkc=1024):
    m = q.shape[0]
    nq = seq // tq
    hb = FLASH_HEADS
    blocks = [_nbytes((tq, hb * A_HEAD_PAD), BF16), _nbytes((seq, hb * A_HEAD_PAD), BF16),
              _nbytes((seq, hb * A_V_PAD), BF16), _nbytes((tq, hb * A_DV), BF16)]
    scratch_shapes = [pltpu.VMEM((hb, seq // tkc, tq, tkc), F32),
                      pltpu.VMEM((hb, tq, V7X_LANES), F32),
                      pltpu.VMEM((hb, tq, A_V_PAD), F32)]
    scratch = _nbytes((hb, tq, seq), F32) + _nbytes((hb, tq, V7X_LANES), F32) + _nbytes((hb, tq, A_V_PAD), F32)
    return pl.pallas_call(
        functools.partial(_flash_body, tq=tq, tkc=tkc),
        grid=(batch, A_HEADS // hb, nq),
        in_specs=[
            pl.BlockSpec((tq, hb * A_HEAD_PAD), lambda b, h, i: (b * nq + i, h)),
            pl.BlockSpec((seq, hb * A_HEAD_PAD), lambda b, h, i: (b, h)),
            pl.BlockSpec((seq, hb * A_V_PAD), lambda b, h, i: (b, h)),
        ],
        out_specs=pl.BlockSpec((tq, hb * A_DV), lambda b, h, i: (b * nq + i, h)),
        out_shape=jax.ShapeDtypeStruct((m, A_HEADS * A_DV), BF16),
        scratch_shapes=scratch_shapes,
        compiler_params=pltpu.CompilerParams(
            dimension_semantics=("parallel", "parallel", "parallel"),
            vmem_limit_bytes=_vmem_limit(blocks, scratch_bytes=scratch,
                                         temp_bytes=hb * 3 * _nbytes((tq, tkc), F32))),
        name="mla_flash",
    )(q, k, v1)


def _pad_lanes(x, width):
    return jnp.pad(x, [(0, 0)] * (x.ndim - 1) + [(0, width - x.shape[-1])])


def _layer(xf, rope, layer, norm_mix, w_in_bf16_stack, b_igate, b_fgate, m_head_norm, w_out_m_stack, q_a_norm,
           w_uq, kv_a_norm, w_ukv, qk_norm_q, qk_norm_k, w_out_a_stack, w_out_stack, norm_mlp, w_up_stack,
           w_down_bf16_stack, *, batch, seq):
    d_model = xf.shape[1]
    q_lora = w_uq.shape[0]
    kv_lora = w_ukv.shape[0]
    m_qk_w = M_HEADS * M_DQK
    m_v_w = M_HEADS * M_DV
    w_in_cols = lambda lo, hi: lax.slice(w_in_bf16_stack, (layer, 0, lo), (layer + 1, d_model, hi))[0]
    o_i = 2 * m_qk_w + 2 * m_v_w
    o_f = o_i + M_HEADS
    o_cq = o_f + M_HEADS
    o_kva = o_cq + q_lora
    o_ga = o_kva + kv_lora + A_ROPE
    w_gates = w_in_cols(o_ga, o_ga + 2 * d_model)
    w_small = jnp.concatenate([
        _pad_lanes(w_in_cols(o_cq, o_ga), q_lora + kv_lora + V7X_LANES),
        _pad_lanes(w_in_cols(o_i, o_f), V7X_LANES),
        _pad_lanes(w_in_cols(o_f, o_cq), V7X_LANES),
    ], axis=1)
    kv_blk = q_lora // kv_lora
    rope_blk = (q_lora + kv_lora) // V7X_LANES
    i_blk = rope_blk + 1
    f_blk = i_blk + 1

    twice_rope = lambda t: jnp.concatenate([t, t[..., A_NOPE:]], axis=-1)
    w_q = twice_rope(w_uq.astype(BF16).reshape(q_lora, A_HEADS, A_QK)).reshape(q_lora, A_HEADS * A_HEAD_PAD)
    gq = twice_rope(qk_norm_q * (A_QK ** -0.5 * LOG2E)).reshape(1, A_HEAD_PAD)
    hpb = 4
    w_kv = w_ukv.astype(BF16)
    gkn = qk_norm_k[:A_NOPE].reshape(1, V7X_LANES)
    gkr = _pad_lanes(qk_norm_k[A_NOPE:], V7X_LANES).reshape(1, V7X_LANES)

    h = _rmsnorm(xf, norm_mix)
    proj = _matmul(h, w_in_bf16_stack, layer=layer, n_cols=o_i, bm=1024, bn=1024, out_dtype=BF16,
                   name="in_proj_main")
    gab = _matmul(h, w_gates, bm=1024, bn=1024, out_dtype=BF16, name="in_proj_gates")
    small = _matmul(h, w_small, bm=1024, bn=w_small.shape[1] // 3, out_dtype=F32, name="in_proj_small")

    bcol, gcol, grow = _mlstm_gates(small, b_igate, b_fgate, i_blk=i_blk, f_blk=f_blk, chunk=MLSTM_CHUNK)
    hm = _mlstm(proj, bcol, gcol, grow, m_head_norm, batch=batch, seq=seq, chunk=MLSTM_CHUNK)

    qh = _qproj(small, q_a_norm, w_q, gq, rope, rank=q_lora, heads_per_block=hpb)
    kh, vh = _kvproj(small, kv_a_norm, w_kv, gkn, gkr, rope, a_blk=kv_blk, rank=kv_lora,
                     rope_blk=rope_blk, heads_per_block=hpb)
    att = _flash_attention(qh, kh, vh, batch=batch, seq=seq)

    bn = 512
    t = _matmul_wres(hm, w_out_m_stack, layer, n_cols=d_model, bm=1024, bn=bn, out_dtype=F32,
                     epilogue=_ep_gate, extras=[(gab, 0)], name="out_proj_mlstm")
    merged = _matmul_wres(att, w_out_a_stack, layer, n_cols=d_model, bm=1024, bn=bn, out_dtype=BF16,
                          epilogue=_ep_gate_add, extras=[(gab, d_model // bn), (t, 0)], name="out_proj_mla")
    xf = _matmul_wres(merged, w_out_stack, layer, n_cols=d_model, bm=1024, bn=bn, out_dtype=F32,
                      epilogue=_ep_residual, extras=[(xf, 0)], name="out_proj")

    h2 = _rmsnorm(xf, norm_mlp)
    u = _matmul_wstat(h2, w_up_stack, layer, bm=1024, bn=1024, out_dtype=BF16, epilogue=_ep_relu2, name="mlp_up")
    return _matmul_kacc(u, w_down_bf16_stack, layer, xf, bm=1024, bn=1024, bk=4096, name="mlp_down")


def kernel(x, positions, norm_mix, w_in, b_igate, b_fgate, m_head_norm, w_out_m, q_a_norm, w_uq,
           kv_a_norm, w_ukv, qk_norm_q, qk_norm_k, w_out_a, w_out, norm_mlp, w_up, w_down):
    batch, seq, d_model = x.shape
    depth = w_in.shape[0]
    m = batch * seq
    inv_freq = ROPE_THETA ** (-jnp.arange(0, A_ROPE, 2, dtype=F32) / A_ROPE)
    ang = positions.astype(F32).reshape(m, 1) * inv_freq
    rope = _rope_tables(_pad_lanes(jnp.concatenate([ang, ang], axis=-1), V7X_LANES))
    xf = x.reshape(m, d_model)
    w_in_bf16 = w_in.astype(BF16)
    w_down_bf16 = w_down.astype(BF16)
    for l in range(depth):
        xf = _layer(xf, rope, l, norm_mix[l], w_in_bf16, b_igate[l], b_fgate[l], m_head_norm[l], w_out_m,
                    q_a_norm[l], w_uq[l], kv_a_norm[l], w_ukv[l], qk_norm_q[l], qk_norm_k[l], w_out_a,
                    w_out, norm_mlp[l], w_up, w_down_bf16, batch=batch, seq=seq)
    return xf.reshape(batch, seq, d_model)
```

```python
import functools
import math

import jax
import jax.numpy as jnp
from jax import lax
from jax.experimental import pallas as pl
from jax.experimental.pallas import tpu as pltpu

F32 = jnp.float32
BF16 = jnp.bfloat16

M_HEADS = 8
M_DQK = 256
M_DV = 512
GATE_SOFTCAP = 15.0
A_HEADS = 32
A_NOPE = 128
A_ROPE = 64
A_QK = A_NOPE + A_ROPE
A_DV = 128
ROPE_THETA = 10000.0
EPS = 1e-6

V7X_LANES = 128
V7X_SUBLANES = 8
V7X_VMEM_BYTES = 64 * 1024 * 1024
V7X_VMEM_REQUEST_CAP = V7X_VMEM_BYTES - 6 * 1024 * 1024

A_HEAD_PAD = 2 * V7X_LANES
A_V_PAD = 2 * A_DV
ROPE_HALF = A_ROPE // 2
MLSTM_CHUNK = 256
LOG2E = math.log2(math.e)


def _vmem_limit(block_bytes, scratch_bytes=0, temp_bytes=0):
    need = 2 * sum(block_bytes) + scratch_bytes + temp_bytes + (2 << 20)
    return int(min(max(need, 16 << 20), V7X_VMEM_REQUEST_CAP))


def _nbytes(shape, dtype):
    return math.prod(shape) * jnp.dtype(dtype).itemsize


def _rmsnorm_body(x_ref, g_ref, o_ref):
    x = x_ref[...]
    ms = jnp.mean(x * x, axis=-1, keepdims=True)
    o_ref[...] = (x * lax.rsqrt(ms + EPS) * g_ref[...]).astype(o_ref.dtype)


def _rmsnorm(x, g, *, tm=256):
    m, d = x.shape
    return pl.pallas_call(
        _rmsnorm_body,
        grid=(m // tm,),
        in_specs=[pl.BlockSpec((tm, d), lambda i: (i, 0)),
                  pl.BlockSpec((1, d), lambda i: (0, 0))],
        out_specs=pl.BlockSpec((tm, d), lambda i: (i, 0)),
        out_shape=jax.ShapeDtypeStruct((m, d), BF16),
        compiler_params=pltpu.CompilerParams(
            dimension_semantics=("parallel",),
            vmem_limit_bytes=_vmem_limit([_nbytes((tm, d), F32), _nbytes((tm, d), BF16)],
                                         temp_bytes=2 * _nbytes((tm, d), F32))),
        name="rmsnorm",
    )(x, g.reshape(1, d))


def _ep_identity(acc):
    return acc


def _ep_relu2(acc):
    return jnp.square(jnp.maximum(acc, 0.0))


def _ep_gate(acc, g):
    return jax.nn.sigmoid(g.astype(F32)) * acc


def _ep_gate_add(acc, g, t):
    return t + jax.nn.sigmoid(g.astype(F32)) * acc


def _ep_residual(acc, r):
    return r + acc


def _mm_body(*refs, n_extra, epilogue):
    a_ref, w_ref = refs[0], refs[1]
    extra_refs = refs[2:2 + n_extra]
    o_ref = refs[2 + n_extra]
    acc = jnp.dot(a_ref[...], w_ref[...], preferred_element_type=F32)
    o_ref[...] = epilogue(acc, *[r[...] for r in extra_refs]).astype(o_ref.dtype)


def _matmul(a, w, *, bm, bn, out_dtype, epilogue=_ep_identity, extras=(), layer=None, n_cols=None, name):
    m, kdim = a.shape
    if layer is None:
        n = w.shape[1]
        w_spec = pl.BlockSpec((kdim, bn), lambda i, j: (0, j))
    else:
        n = n_cols
        w_spec = pl.BlockSpec((None, kdim, bn), lambda i, j: (layer, 0, j))
    in_specs = [pl.BlockSpec((bm, kdim), lambda i, j: (i, 0)), w_spec]
    blocks = [_nbytes((bm, kdim), a.dtype), _nbytes((kdim, bn), w.dtype), _nbytes((bm, bn), out_dtype)]
    for arr, off in extras:
        in_specs.append(pl.BlockSpec((bm, bn), lambda i, j, off=off: (i, off + j)))
        blocks.append(_nbytes((bm, bn), arr.dtype))
    return pl.pallas_call(
        functools.partial(_mm_body, n_extra=len(extras), epilogue=epilogue),
        grid=(m // bm, n // bn),
        in_specs=in_specs,
        out_specs=pl.BlockSpec((bm, bn), lambda i, j: (i, j)),
        out_shape=jax.ShapeDtypeStruct((m, n), out_dtype),
        compiler_params=pltpu.CompilerParams(
            dimension_semantics=("parallel", "parallel"),
            vmem_limit_bytes=_vmem_limit(blocks, temp_bytes=3 * _nbytes((bm, bn), F32))),
        name=name,
    )(a, w, *[arr for arr, _ in extras])


def _mm_kacc_body(a_ref, w_ref, r_ref, o_ref):
    part = jnp.dot(a_ref[...], w_ref[...], preferred_element_type=F32)

    @pl.when(pl.program_id(2) == 0)
    def _():
        o_ref[...] = r_ref[...] + part

    @pl.when(pl.program_id(2) > 0)
    def _():
        o_ref[...] += part


def _matmul_kacc(a, w_stack, layer, res, *, bm, bn, bk, name):
    m, kdim = a.shape
    n = w_stack.shape[2]
    blocks = [_nbytes((bm, bk), a.dtype), _nbytes((bk, bn), w_stack.dtype), 2 * _nbytes((bm, bn), F32)]
    return pl.pallas_call(
        _mm_kacc_body,
        grid=(m // bm, n // bn, kdim // bk),
        in_specs=[pl.BlockSpec((bm, bk), lambda i, j, k: (i, k)),
                  pl.BlockSpec((None, bk, bn), lambda i, j, k: (layer, k, j)),
                  pl.BlockSpec((bm, bn), lambda i, j, k: (i, j))],
        out_specs=pl.BlockSpec((bm, bn), lambda i, j, k: (i, j)),
        out_shape=jax.ShapeDtypeStruct((m, n), F32),
        compiler_params=pltpu.CompilerParams(
            dimension_semantics=("parallel", "parallel", "arbitrary"),
            vmem_limit_bytes=_vmem_limit(blocks, temp_bytes=2 * _nbytes((bm, bn), F32))),
        name=name,
    )(a, w_stack, res)


def _mm_wstat_body(*refs, n_extra, epilogue, n_col_blocks):
    a_ref, wn_ref = refs[0], refs[1]
    extra_refs = refs[2:2 + n_extra]
    o_ref = refs[2 + n_extra]
    wb_ref = refs[3 + n_extra]
    p = pl.program_id(0)
    i = pl.program_id(1)
    fill = p % 2
    kc = wn_ref.shape[0]

    @pl.when(p < n_col_blocks)
    def _():
        wb_ref[fill, pl.ds(pl.multiple_of(i * kc, kc), kc), :] = wn_ref[...].astype(BF16)

    @pl.when(p > 0)
    def _():
        acc = jnp.dot(a_ref[...], wb_ref[1 - fill], preferred_element_type=F32)
        o_ref[...] = epilogue(acc, *[r[...] for r in extra_refs]).astype(o_ref.dtype)


def _matmul_wstat(a, w_stack, layer, *, bm, bn, out_dtype, epilogue=_ep_identity, extras=(), name):
    m, kdim = a.shape
    n = w_stack.shape[2]
    n_row_blocks = m // bm
    n_col_blocks = n // bn
    kc = kdim // n_row_blocks
    row = lambda p, i: jnp.where(p == 0, 0, i)
    col = lambda p: jnp.maximum(p - 1, 0)
    in_specs = [pl.BlockSpec((bm, kdim), lambda p, i: (row(p, i), 0)),
                pl.BlockSpec((None, kc, bn), lambda p, i: (layer, i, jnp.minimum(p, n_col_blocks - 1)))]
    blocks = [_nbytes((bm, kdim), a.dtype), _nbytes((kc, bn), F32), _nbytes((bm, bn), out_dtype)]
    for arr, off in extras:
        in_specs.append(pl.BlockSpec((bm, bn), lambda p, i, off=off: (row(p, i), off + col(p))))
        blocks.append(_nbytes((bm, bn), arr.dtype))
    return pl.pallas_call(
        functools.partial(_mm_wstat_body, n_extra=len(extras), epilogue=epilogue, n_col_blocks=n_col_blocks),
        grid=(n_col_blocks + 1, n_row_blocks),
        in_specs=in_specs,
        out_specs=pl.BlockSpec((bm, bn), lambda p, i: (row(p, i), col(p))),
        out_shape=jax.ShapeDtypeStruct((m, n), out_dtype),
        scratch_shapes=[pltpu.VMEM((2, kdim, bn), BF16)],
        compiler_params=pltpu.CompilerParams(
            dimension_semantics=("arbitrary", "arbitrary"),
            vmem_limit_bytes=_vmem_limit(blocks, scratch_bytes=2 * _nbytes((kdim, bn), BF16),
                                         temp_bytes=3 * _nbytes((bm, bn), F32))),
        name=name,
    )(a, w_stack, *[arr for arr, _ in extras])


def _rope_body(ang_ref, cmask_ref, lo_sign_ref, hi_sign_ref, cos_ref, sin_lo_ref, sin_hi_ref):
    ang = ang_ref[...]
    sin = jnp.sin(ang)
    cos_ref[...] = jnp.cos(ang) * cmask_ref[...]
    sin_lo_ref[...] = sin * lo_sign_ref[...]
    sin_hi_ref[...] = sin * hi_sign_ref[...]


def _rope_tables(ang128, *, tm=1024):
    m = ang128.shape[0]
    lane = jnp.arange(V7X_LANES)
    lo = (lane < ROPE_HALF).astype(F32).reshape(1, V7X_LANES)
    hi = jnp.logical_and(lane >= ROPE_HALF, lane < A_ROPE).astype(F32).reshape(1, V7X_LANES)
    row = pl.BlockSpec((tm, V7X_LANES), lambda i: (i, 0))
    vec = pl.BlockSpec((1, V7X_LANES), lambda i: (0, 0))
    return pl.pallas_call(
        _rope_body,
        grid=(m // tm,),
        in_specs=[row, vec, vec, vec],
        out_specs=[row, row, row],
        out_shape=[jax.ShapeDtypeStruct((m, V7X_LANES), F32)] * 3,
        compiler_params=pltpu.CompilerParams(dimension_semantics=("parallel",)),
        name="rope_tables",
    )(ang128, lo + hi, -lo, hi)


def _rotate(x, rope):
    cos, sin_lo, sin_hi = rope
    hi_on_lo = pltpu.roll(x, shift=V7X_LANES - ROPE_HALF, axis=1)
    lo_on_hi = pltpu.roll(x, shift=ROPE_HALF, axis=1)
    return x * cos + hi_on_lo * sin_lo + lo_on_hi * sin_hi


def _split_bf16(x):
    hi = x.astype(BF16)
    r1 = x - hi.astype(F32)
    mid = r1.astype(BF16)
    lo = (r1 - mid.astype(F32)).astype(BF16)
    return hi, mid, lo


def _gate_body(i_ref, f_ref, bi_ref, bf_ref, bcol_ref, gcol_ref, grow_ref, *, chunk):
    li = GATE_SOFTCAP * jnp.tanh((i_ref[...] + bi_ref[...]) / GATE_SOFTCAP)
    z = GATE_SOFTCAP * jnp.tanh((f_ref[...] + bf_ref[...]) / GATE_SOFTCAP)
    lf = jnp.minimum(z, 0.0) - jnp.log(1.0 + jnp.exp(-jnp.abs(z)))
    row = lax.broadcasted_iota(jnp.int32, (chunk, chunk), 0)
    col = lax.broadcasted_iota(jnp.int32, (chunk, chunk), 1)
    tril = (col <= row).astype(BF16)
    b = sum(jnp.dot(tril, piece, preferred_element_type=F32) for piece in _split_bf16(lf))
    g = li - b
    bcol_ref[...] = b
    gcol_ref[...] = g
    grow_ref[...] = g.T[:V7X_SUBLANES, :]


def _mlstm_gates(small, b_i, b_f, *, i_blk, f_blk, chunk):
    m = small.shape[0]
    pad = lambda b: jnp.pad(b.astype(F32), (0, V7X_LANES - b.shape[0])).reshape(1, V7X_LANES)
    col = pl.BlockSpec((chunk, V7X_LANES), lambda c: (c, 0))
    vec = pl.BlockSpec((1, V7X_LANES), lambda c: (0, 0))
    return pl.pallas_call(
        functools.partial(_gate_body, chunk=chunk),
        grid=(m // chunk,),
        in_specs=[pl.BlockSpec((chunk, V7X_LANES), lambda c: (c, i_blk)),
                  pl.BlockSpec((chunk, V7X_LANES), lambda c: (c, f_blk)),
                  vec, vec],
        out_specs=[col, col, pl.BlockSpec((V7X_SUBLANES, chunk), lambda c: (0, c))],
        out_shape=[jax.ShapeDtypeStruct((m, V7X_LANES), F32),
                   jax.ShapeDtypeStruct((m, V7X_LANES), F32),
                   jax.ShapeDtypeStruct((V7X_SUBLANES, m), F32)],
        compiler_params=pltpu.CompilerParams(dimension_semantics=("parallel",)),
        name="mlstm_gates",
    )(small, small, pad(b_i), pad(b_f))


def _mlstm_body(q_ref, k_ref, v_ref, o_ref, bcol_ref, gcol_ref, grow_ref, hn_ref, out_ref,
                c_ref, n_ref, m_ref, *, chunk):
    head = pl.program_id(1)

    @pl.when(pl.program_id(2) == 0)
    def _():
        c_ref[...] = jnp.zeros_like(c_ref)
        n_ref[...] = jnp.zeros_like(n_ref)
        m_ref[...] = jnp.zeros_like(m_ref)

    lane_sel = lax.broadcasted_iota(jnp.int32, (1, V7X_LANES), 1) == head
    bcol = jnp.sum(jnp.where(lane_sel, bcol_ref[...], 0.0), axis=-1, keepdims=True)
    gcol = jnp.sum(jnp.where(lane_sel, gcol_ref[...], 0.0), axis=-1, keepdims=True)
    sub_sel = lax.broadcasted_iota(jnp.int32, (V7X_SUBLANES, 1), 0) == head
    grow = jnp.sum(jnp.where(sub_sel, grow_ref[...], 0.0), axis=0, keepdims=True)

    m_prev = m_ref[...]
    row = lax.broadcasted_iota(jnp.int32, (chunk, chunk), 0)
    col = lax.broadcasted_iota(jnp.int32, (chunk, chunk), 1)
    dmat = jnp.where(col <= row, bcol + grow, -jnp.inf)
    inter = bcol + m_prev
    mj = jnp.maximum(inter, jnp.max(dmat, axis=-1, keepdims=True))
    w_inter = jnp.exp(inter - mj)

    q = q_ref[...]
    k = k_ref[...]
    v = v_ref[...]
    scale = M_DQK ** -0.5
    qk = lax.dot_general(q, k, (((1,), (1,)), ((), ())), preferred_element_type=F32) * scale
    p = jnp.exp(dmat - mj) * qk
    cq = jnp.dot(q, c_ref[...].astype(BF16), preferred_element_type=F32) * scale
    num = w_inter * cq + jnp.dot(p.astype(BF16), v, preferred_element_type=F32)
    qn = jnp.sum(q.astype(F32) * n_ref[...], axis=-1, keepdims=True) * scale
    nq = w_inter * qn + jnp.sum(p, axis=-1, keepdims=True)
    hv = num / jnp.maximum(jnp.abs(nq), jnp.exp(-mj))

    ms = jnp.mean(hv * hv, axis=-1, keepdims=True)
    hn = hv * lax.rsqrt(ms + EPS) * hn_ref[...]
    out_ref[...] = (hn * jax.nn.sigmoid(o_ref[...].astype(F32))).astype(out_ref.dtype)

    b_last = bcol[chunk - 1:chunk, :]
    acol = b_last + gcol
    m_new = jnp.maximum(b_last + m_prev, jnp.max(acol, axis=0, keepdims=True))
    decay = jnp.exp(b_last + m_prev - m_new)
    kw = k.astype(F32) * jnp.exp(acol - m_new)
    c_ref[...] = decay * c_ref[...] + lax.dot_general(
        kw.astype(BF16), v, (((0,), (0,)), ((), ())), preferred_element_type=F32)
    n_ref[...] = decay * n_ref[...] + jnp.sum(kw, axis=0, keepdims=True)
    m_ref[...] = m_new


def _mlstm(proj, bcol, gcol, grow, head_gain, *, batch, seq, chunk):
    m = proj.shape[0]
    nc = seq // chunk
    k_blk0 = (M_HEADS * M_DQK) // M_DQK
    v_blk0 = (2 * M_HEADS * M_DQK) // M_DV
    o_blk0 = v_blk0 + M_HEADS
    rows = lambda b, h, c: b * nc + c
    return pl.pallas_call(
        functools.partial(_mlstm_body, chunk=chunk),
        grid=(batch, M_HEADS, nc),
        in_specs=[
            pl.BlockSpec((chunk, M_DQK), lambda b, h, c: (rows(b, h, c), h)),
            pl.BlockSpec((chunk, M_DQK), lambda b, h, c: (rows(b, h, c), k_blk0 + h)),
            pl.BlockSpec((chunk, M_DV), lambda b, h, c: (rows(b, h, c), v_blk0 + h)),
            pl.BlockSpec((chunk, M_DV), lambda b, h, c: (rows(b, h, c), o_blk0 + h)),
            pl.BlockSpec((chunk, V7X_LANES), lambda b, h, c: (rows(b, h, c), 0)),
            pl.BlockSpec((chunk, V7X_LANES), lambda b, h, c: (rows(b, h, c), 0)),
            pl.BlockSpec((V7X_SUBLANES, chunk), lambda b, h, c: (0, rows(b, h, c))),
            pl.BlockSpec((1, M_DV), lambda b, h, c: (0, h)),
        ],
        out_specs=pl.BlockSpec((chunk, M_DV), lambda b, h, c: (rows(b, h, c), h)),
        out_shape=jax.ShapeDtypeStruct((m, M_HEADS * M_DV), BF16),
        scratch_shapes=[pltpu.VMEM((M_DQK, M_DV), F32),
                        pltpu.VMEM((1, M_DQK), F32),
                        pltpu.VMEM((1, 1), F32)],
        compiler_params=pltpu.CompilerParams(
            dimension_semantics=("parallel", "parallel", "arbitrary")),
        name="mlstm",
    )(proj, proj, proj, proj, bcol, gcol, grow, head_gain.reshape(1, M_HEADS * M_DV))


def _row_rmsnorm(a, gain):
    ms = jnp.mean(a * a, axis=-1, keepdims=True)
    return (a * lax.rsqrt(ms + EPS) * gain).astype(BF16)


def _qproj_body(a_ref, an_ref, w_ref, gq_ref, cos_ref, sin_lo_ref, sin_hi_ref, o_ref, *, heads_per_block):
    a = _row_rmsnorm(a_ref[...], an_ref[...])
    cos = cos_ref[...]
    sin = sin_lo_ref[...] + sin_hi_ref[...]
    gq = gq_ref[...]
    real = lax.broadcasted_iota(jnp.int32, (1, A_HEAD_PAD), 1) < A_QK
    half = a.shape[0] // 2
    for r in range(2):
        rows = slice(r * half, (r + 1) * half)
        acc = jnp.dot(a[rows], w_ref[...], preferred_element_type=F32)
        for hh in range(heads_per_block):
            blk = acc[:, hh * A_HEAD_PAD:(hh + 1) * A_HEAD_PAD]
            ss = jnp.sum(jnp.where(real, blk * blk, 0.0), axis=-1, keepdims=True)
            qn = blk * lax.rsqrt(ss * (1.0 / A_QK) + EPS) * gq
            rope = qn[:, V7X_LANES:]
            rope = rope * cos[rows] + pltpu.roll(rope, shift=V7X_LANES - ROPE_HALF, axis=1) * sin[rows]
            o_ref[rows, hh * A_HEAD_PAD:hh * A_HEAD_PAD + V7X_LANES] = qn[:, :V7X_LANES].astype(o_ref.dtype)
            o_ref[rows, hh * A_HEAD_PAD + V7X_LANES:(hh + 1) * A_HEAD_PAD] = rope.astype(o_ref.dtype)


def _qproj(small, a_norm, w_q, gq, rope, *, rank, bm=1024, heads_per_block=4):
    m = small.shape[0]
    bn = heads_per_block * A_HEAD_PAD
    n = w_q.shape[1]
    rowvec = pl.BlockSpec((bm, V7X_LANES), lambda i, j: (i, 0))
    return pl.pallas_call(
        functools.partial(_qproj_body, heads_per_block=heads_per_block),
        grid=(m // bm, n // bn),
        in_specs=[pl.BlockSpec((bm, rank), lambda i, j: (i, 0)),
                  pl.BlockSpec((1, rank), lambda i, j: (0, 0)),
                  pl.BlockSpec((rank, bn), lambda i, j: (0, j)),
                  pl.BlockSpec((1, A_HEAD_PAD), lambda i, j: (0, 0)),
                  rowvec, rowvec, rowvec],
        out_specs=pl.BlockSpec((bm, bn), lambda i, j: (i, j)),
        out_shape=jax.ShapeDtypeStruct((m, n), BF16),
        compiler_params=pltpu.CompilerParams(
            dimension_semantics=("parallel", "parallel"),
            vmem_limit_bytes=_vmem_limit([_nbytes((bm, rank), F32), _nbytes((rank, bn), BF16),
                                          _nbytes((bm, bn), BF16)],
                                         temp_bytes=4 * _nbytes((bm, bn), F32))),
        name="mla_qproj",
    )(small, a_norm.reshape(1, rank), w_q, gq, *rope)


def _kvproj_body(a_ref, an_ref, w_ref, kr_ref, gkn_ref, gkr_ref, cos_ref, sin_lo_ref, sin_hi_ref, k_ref, v_ref, *,
                 heads_per_block):
    a = _row_rmsnorm(a_ref[...], an_ref[...])
    kr = kr_ref[...]
    ss_rope = jnp.sum(kr * kr, axis=-1, keepdims=True)
    k_rot = _rotate(kr * gkr_ref[...], (cos_ref[...], sin_lo_ref[...], sin_hi_ref[...]))
    gkn = gkn_ref[...]
    half = a.shape[0] // 2
    ones = jnp.ones((half, A_DV), v_ref.dtype)
    for r in range(2):
        rows = slice(r * half, (r + 1) * half)
        acc = jnp.dot(a[rows], w_ref[...], preferred_element_type=F32)
        for hh in range(heads_per_block):
            kv = acc[:, hh * (A_NOPE + A_DV):(hh + 1) * (A_NOPE + A_DV)]
            kn = kv[:, :A_NOPE]
            ss = jnp.sum(kn * kn, axis=-1, keepdims=True) + ss_rope[rows]
            rs = lax.rsqrt(ss * (1.0 / A_QK) + EPS)
            k_ref[rows, hh * A_HEAD_PAD:hh * A_HEAD_PAD + V7X_LANES] = (kn * rs * gkn).astype(k_ref.dtype)
            k_ref[rows, hh * A_HEAD_PAD + V7X_LANES:(hh + 1) * A_HEAD_PAD] = (k_rot[rows] * rs).astype(k_ref.dtype)
            v_ref[rows, hh * A_V_PAD:hh * A_V_PAD + A_DV] = kv[:, A_NOPE:].astype(v_ref.dtype)
            v_ref[rows, hh * A_V_PAD + A_DV:(hh + 1) * A_V_PAD] = ones


def _kvproj(small, a_norm, w_kv, gkn, gkr, rope, *, a_blk, rank, rope_blk, bm=1024, heads_per_block=4):
    m = small.shape[0]
    bn = heads_per_block * (A_NOPE + A_DV)
    n_blocks = w_kv.shape[1] // bn
    rowvec = pl.BlockSpec((bm, V7X_LANES), lambda i, j: (i, 0))
    lanevec = pl.BlockSpec((1, V7X_LANES), lambda i, j: (0, 0))
    return pl.pallas_call(
        functools.partial(_kvproj_body, heads_per_block=heads_per_block),
        grid=(m // bm, n_blocks),
        in_specs=[pl.BlockSpec((bm, rank), lambda i, j: (i, a_blk)),
                  pl.BlockSpec((1, rank), lambda i, j: (0, 0)),
                  pl.BlockSpec((rank, bn), lambda i, j: (0, j)),
                  pl.BlockSpec((bm, V7X_LANES), lambda i, j: (i, rope_blk)),
                  lanevec, lanevec, rowvec, rowvec, rowvec],
        out_specs=[pl.BlockSpec((bm, heads_per_block * A_HEAD_PAD), lambda i, j: (i, j)),
                   pl.BlockSpec((bm, heads_per_block * A_V_PAD), lambda i, j: (i, j))],
        out_shape=[jax.ShapeDtypeStruct((m, A_HEADS * A_HEAD_PAD), BF16),
                   jax.ShapeDtypeStruct((m, A_HEADS * A_V_PAD), BF16)],
        compiler_params=pltpu.CompilerParams(
            dimension_semantics=("parallel", "parallel"),
            vmem_limit_bytes=_vmem_limit([_nbytes((bm, rank), F32), _nbytes((rank, bn), BF16),
                                          _nbytes((bm, bn), BF16), _nbytes((bm, bn), BF16)],
                                         temp_bytes=4 * _nbytes((bm, bn), F32))),
        name="mla_kvproj",
    )(small, a_norm.reshape(1, rank), w_kv, small, gkn, gkr, *rope)


FLASH_HEADS = 2


def _flash_body(q_ref, k_ref, v_ref, o_ref, s_ref, mx_ref, acc_ref, *, tq, tkc):
    i = pl.program_id(2)
    n_slab = tkc // V7X_LANES
    n_unmasked = (i * tq) // tkc
    n_chunks = ((i + 1) * tq + tkc - 1) // tkc
    mx_ref[...] = jnp.full_like(mx_ref, -jnp.inf)
    acc_ref[...] = jnp.zeros_like(acc_ref)

    def scores(c, masked):
        rows = pl.ds(pl.multiple_of(c * tkc, tkc), tkc)
        for e in range(FLASH_HEADS):
            s = lax.dot_general(q_ref[:, e * A_HEAD_PAD:(e + 1) * A_HEAD_PAD],
                                k_ref[rows, e * A_HEAD_PAD:(e + 1) * A_HEAD_PAD],
                                (((1,), (1,)), ((), ())), preferred_element_type=F32)
            if masked:
                row = lax.broadcasted_iota(jnp.int32, (tq, tkc), 0)
                col = lax.broadcasted_iota(jnp.int32, (tq, tkc), 1)
                s = jnp.where(col + c * tkc <= row + i * tq, s, -jnp.inf)
            s_ref[e, c] = s
            mx = mx_ref[e]
            for t in range(n_slab):
                mx = jnp.maximum(mx, s[:, t * V7X_LANES:(t + 1) * V7X_LANES])
            mx_ref[e] = mx

    def unmasked_chunk(c, carry):
        scores(c, False)
        return carry

    def masked_chunk(c, carry):
        scores(c, True)
        return carry

    lax.fori_loop(0, n_unmasked, unmasked_chunk, 0)
    lax.fori_loop(n_unmasked, n_chunks, masked_chunk, 0)

    for e in range(FLASH_HEADS):
        mx_ref[e] = jnp.broadcast_to(jnp.max(mx_ref[e], axis=-1, keepdims=True), (tq, V7X_LANES))

    def probs_chunk(c, carry):
        rows = pl.ds(pl.multiple_of(c * tkc, tkc), tkc)
        for e in range(FLASH_HEADS):
            s = s_ref[e, c]
            mx = mx_ref[e]
            p = jnp.concatenate([jnp.exp2(s[:, t * V7X_LANES:(t + 1) * V7X_LANES] - mx) for t in range(n_slab)],
                                axis=-1).astype(BF16)
            acc_ref[e] += jnp.dot(p, v_ref[rows, e * A_V_PAD:(e + 1) * A_V_PAD], preferred_element_type=F32)
        return carry

    lax.fori_loop(0, n_chunks, probs_chunk, 0)
    for e in range(FLASH_HEADS):
        acc = acc_ref[e]
        o_ref[:, e * A_DV:(e + 1) * A_DV] = (acc[:, :A_DV] / acc[:, A_DV:]).astype(o_ref.dtype)


def _flash_attention(q, k, v1, *, batch, seq, tq=512, tkc=1024):
    m = q.shape[0]
    nq = seq // tq
    hb = FLASH_HEADS
    blocks = [_nbytes((tq, hb * A_HEAD_PAD), BF16), _nbytes((seq, hb * A_HEAD_PAD), BF16),
              _nbytes((seq, hb * A_V_PAD), BF16), _nbytes((tq, hb * A_DV), BF16)]
    scratch_shapes = [pltpu.VMEM((hb, seq // tkc, tq, tkc), F32),
                      pltpu.VMEM((hb, tq, V7X_LANES), F32),
                      pltpu.VMEM((hb, tq, A_V_PAD), F32)]
    scratch = _nbytes((hb, tq, seq), F32) + _nbytes((hb, tq, V7X_LANES), F32) + _nbytes((hb, tq, A_V_PAD), F32)
    return pl.pallas_call(
        functools.partial(_flash_body, tq=tq, tkc=tkc),
        grid=(batch, A_HEADS // hb, nq),
        in_specs=[
            pl.BlockSpec((tq, hb * A_HEAD_PAD), lambda b, h, i: (b * nq + i, h)),
            pl.BlockSpec((seq, hb * A_HEAD_PAD), lambda b, h, i: (b, h)),
            pl.BlockSpec((seq, hb * A_V_PAD), lambda b, h, i: (b, h)),
        ],
        out_specs=pl.BlockSpec((tq, hb * A_DV), lambda b, h, i: (b * nq + i, h)),
        out_shape=jax.ShapeDtypeStruct((m, A_HEADS * A_DV), BF16),
        scratch_shapes=scratch_shapes,
        compiler_params=pltpu.CompilerParams(
            dimension_semantics=("parallel", "parallel", "parallel"),
            vmem_limit_bytes=_vmem_limit(blocks, scratch_bytes=scratch,
                                         temp_bytes=hb * 3 * _nbytes((tq, tkc), F32))),
        name="mla_flash",
    )(q, k, v1)


def _pad_lanes(x, width):
    return jnp.pad(x, [(0, 0)] * (x.ndim - 1) + [(0, width - x.shape[-1])])


def _layer(xf, rope, layer, norm_mix, w_in_bf16_stack, b_igate, b_fgate, m_head_norm, w_out_m_stack, q_a_norm,
           w_uq, kv_a_norm, w_ukv, qk_norm_q, qk_norm_k, w_out_a_stack, w_out_stack, norm_mlp, w_up_stack,
           w_down_bf16_stack, *, batch, seq):
    d_model = xf.shape[1]
    q_lora = w_uq.shape[0]
    kv_lora = w_ukv.shape[0]
    m_qk_w = M_HEADS * M_DQK
    m_v_w = M_HEADS * M_DV
    w_in_cols = lambda lo, hi: lax.slice(w_in_bf16_stack, (layer, 0, lo), (layer + 1, d_model, hi))[0]
    o_i = 2 * m_qk_w + 2 * m_v_w
    o_f = o_i + M_HEADS
    o_cq = o_f + M_HEADS
    o_kva = o_cq + q_lora
    o_ga = o_kva + kv_lora + A_ROPE
    w_gates = w_in_cols(o_ga, o_ga + 2 * d_model)
    w_small = jnp.concatenate([
        _pad_lanes(w_in_cols(o_cq, o_ga), q_lora + kv_lora + V7X_LANES),
        _pad_lanes(w_in_cols(o_i, o_f), V7X_LANES),
        _pad_lanes(w_in_cols(o_f, o_cq), V7X_LANES),
    ], axis=1)
    kv_blk = q_lora // kv_lora
    rope_blk = (q_lora + kv_lora) // V7X_LANES
    i_blk = rope_blk + 1
    f_blk = i_blk + 1

    twice_rope = lambda t: jnp.concatenate([t, t[..., A_NOPE:]], axis=-1)
    w_q = twice_rope(w_uq.astype(BF16).reshape(q_lora, A_HEADS, A_QK)).reshape(q_lora, A_HEADS * A_HEAD_PAD)
    gq = twice_rope(qk_norm_q * (A_QK ** -0.5 * LOG2E)).reshape(1, A_HEAD_PAD)
    hpb = 4
    w_kv = w_ukv.astype(BF16)
    gkn = qk_norm_k[:A_NOPE].reshape(1, V7X_LANES)
    gkr = _pad_lanes(qk_norm_k[A_NOPE:], V7X_LANES).reshape(1, V7X_LANES)

    h = _rmsnorm(xf, norm_mix)
    proj = _matmul(h, w_in_bf16_stack, layer=layer, n_cols=o_i, bm=1024, bn=1024, out_dtype=BF16,
                   name="in_proj_main")
    gab = _matmul(h, w_gates, bm=1024, bn=1024, out_dtype=BF16, name="in_proj_gates")
    small = _matmul(h, w_small, bm=1024, bn=w_small.shape[1] // 3, out_dtype=F32, name="in_proj_small")

    bcol, gcol, grow = _mlstm_gates(small, b_igate, b_fgate, i_blk=i_blk, f_blk=f_blk, chunk=MLSTM_CHUNK)
    hm = _mlstm(proj, bcol, gcol, grow, m_head_norm, batch=batch, seq=seq, chunk=MLSTM_CHUNK)

    qh = _qproj(small, q_a_norm, w_q, gq, rope, rank=q_lora, heads_per_block=hpb)
    kh, vh = _kvproj(small, kv_a_norm, w_kv, gkn, gkr, rope, a_blk=kv_blk, rank=kv_lora,
                     rope_blk=rope_blk, heads_per_block=hpb)
    att = _flash_attention(qh, kh, vh, batch=batch, seq=seq)

    bn = 1024
    t = _matmul_wstat(hm, w_out_m_stack, layer, bm=512, bn=bn, out_dtype=F32,
                      epilogue=_ep_gate, extras=[(gab, 0)], name="out_proj_mlstm")
    merged = _matmul_wstat(att, w_out_a_stack, layer, bm=512, bn=bn, out_dtype=BF16,
                           epilogue=_ep_gate_add, extras=[(gab, d_model // bn), (t, 0)], name="out_proj_mla")
    xf = _matmul_wstat(merged, w_out_stack, layer, bm=512, bn=bn, out_dtype=F32,
                       epilogue=_ep_residual, extras=[(xf, 0)], name="out_proj")

    h2 = _rmsnorm(xf, norm_mlp)
    u = _matmul_wstat(h2, w_up_stack, layer, bm=1024, bn=1024, out_dtype=BF16, epilogue=_ep_relu2, name="mlp_up")
    return _matmul_kacc(u, w_down_bf16_stack, layer, xf, bm=1024, bn=1024, bk=4096, name="mlp_down")


def kernel(x, positions, norm_mix, w_in, b_igate, b_fgate, m_head_norm, w_out_m, q_a_norm, w_uq,
           kv_a_norm, w_ukv, qk_norm_q, qk_norm_k, w_out_a, w_out, norm_mlp, w_up, w_down):
    batch, seq, d_model = x.shape
    depth = w_in.shape[0]
    m = batch * seq
    inv_freq = ROPE_THETA ** (-jnp.arange(0, A_ROPE, 2, dtype=F32) / A_ROPE)
    ang = positions.astype(F32).reshape(m, 1) * inv_freq
    rope = _rope_tables(_pad_lanes(jnp.concatenate([ang, ang], axis=-1), V7X_LANES))
    xf = x.reshape(m, d_model)
    w_in_bf16 = w_in.astype(BF16)
    w_down_bf16 = w_down.astype(BF16)
    for l in range(depth):
        xf = _layer(xf, rope, l, norm_mix[l], w_in_bf16, b_igate[l], b_fgate[l], m_head_norm[l], w_out_m,
                    q_a_norm[l], w_uq[l], kv_a_norm[l], w_ukv[l], qk_norm_q[l], qk_norm_k[l], w_out_a,
                    w_out, norm_mlp[l], w_up, w_down_bf16, batch=batch, seq=seq)
    return xf.reshape(batch, seq, d_model)
```

```python
import functools
import math

import jax
import jax.numpy as jnp
from jax import lax
from jax.experimental import pallas as pl
from jax.experimental.pallas import tpu as pltpu

F32 = jnp.float32
BF16 = jnp.bfloat16

M_HEADS = 8
M_DQK = 256
M_DV = 512
GATE_SOFTCAP = 15.0
A_HEADS = 32
A_NOPE = 128
A_ROPE = 64
A_QK = A_NOPE + A_ROPE
A_DV = 128
ROPE_THETA = 10000.0
EPS = 1e-6

V7X_LANES = 128
V7X_SUBLANES = 8
V7X_VMEM_BYTES = 64 * 1024 * 1024
V7X_VMEM_REQUEST_CAP = V7X_VMEM_BYTES - 6 * 1024 * 1024

A_HEAD_PAD = 2 * V7X_LANES
A_V_PAD = 2 * A_DV
ROPE_HALF = A_ROPE // 2
MLSTM_CHUNK = 256
LOG2E = math.log2(math.e)


def _vmem_limit(block_bytes, scratch_bytes=0, temp_bytes=0):
    need = 2 * sum(block_bytes) + scratch_bytes + temp_bytes + (2 << 20)
    return int(min(max(need, 16 << 20), V7X_VMEM_REQUEST_CAP))


def _nbytes(shape, dtype):
    return math.prod(shape) * jnp.dtype(dtype).itemsize


def _rmsnorm_body(x_ref, g_ref, o_ref):
    x = x_ref[...]
    ms = jnp.mean(x * x, axis=-1, keepdims=True)
    o_ref[...] = (x * lax.rsqrt(ms + EPS) * g_ref[...]).astype(o_ref.dtype)


def _rmsnorm(x, g, *, tm=256):
    m, d = x.shape
    return pl.pallas_call(
        _rmsnorm_body,
        grid=(m // tm,),
        in_specs=[pl.BlockSpec((tm, d), lambda i: (i, 0)),
                  pl.BlockSpec((1, d), lambda i: (0, 0))],
        out_specs=pl.BlockSpec((tm, d), lambda i: (i, 0)),
        out_shape=jax.ShapeDtypeStruct((m, d), BF16),
        compiler_params=pltpu.CompilerParams(
            dimension_semantics=("parallel",),
            vmem_limit_bytes=_vmem_limit([_nbytes((tm, d), F32), _nbytes((tm, d), BF16)],
                                         temp_bytes=2 * _nbytes((tm, d), F32))),
        name="rmsnorm",
    )(x, g.reshape(1, d))


def _ep_identity(acc):
    return acc


def _ep_relu2(acc):
    return jnp.square(jnp.maximum(acc, 0.0))


def _ep_gate(acc, g):
    return jax.nn.sigmoid(g.astype(F32)) * acc


def _ep_gate_add(acc, g, t):
    return t + jax.nn.sigmoid(g.astype(F32)) * acc


def _ep_residual(acc, r):
    return r + acc


def _mm_body(*refs, n_extra, epilogue):
    a_ref, w_ref = refs[0], refs[1]
    extra_refs = refs[2:2 + n_extra]
    o_ref = refs[2 + n_extra]
    acc = jnp.dot(a_ref[...], w_ref[...], preferred_element_type=F32)
    o_ref[...] = epilogue(acc, *[r[...] for r in extra_refs]).astype(o_ref.dtype)


def _matmul(a, w, *, bm, bn, out_dtype, epilogue=_ep_identity, extras=(), layer=None, n_cols=None, name):
    m, kdim = a.shape
    if layer is None:
        n = w.shape[1]
        w_spec = pl.BlockSpec((kdim, bn), lambda i, j: (0, j))
    else:
        n = n_cols
        w_spec = pl.BlockSpec((None, kdim, bn), lambda i, j: (layer, 0, j))
    in_specs = [pl.BlockSpec((bm, kdim), lambda i, j: (i, 0)), w_spec]
    blocks = [_nbytes((bm, kdim), a.dtype), _nbytes((kdim, bn), w.dtype), _nbytes((bm, bn), out_dtype)]
    for arr, off in extras:
        in_specs.append(pl.BlockSpec((bm, bn), lambda i, j, off=off: (i, off + j)))
        blocks.append(_nbytes((bm, bn), arr.dtype))
    return pl.pallas_call(
        functools.partial(_mm_body, n_extra=len(extras), epilogue=epilogue),
        grid=(m // bm, n // bn),
        in_specs=in_specs,
        out_specs=pl.BlockSpec((bm, bn), lambda i, j: (i, j)),
        out_shape=jax.ShapeDtypeStruct((m, n), out_dtype),
        compiler_params=pltpu.CompilerParams(
            dimension_semantics=("parallel", "parallel"),
            vmem_limit_bytes=_vmem_limit(blocks, temp_bytes=3 * _nbytes((bm, bn), F32))),
        name=name,
    )(a, w, *[arr for arr, _ in extras])


def _mm_kacc_body(a_ref, w_ref, r_ref, o_ref):
    @pl.when(pl.program_id(2) == 0)
    def _():
        o_ref[...] = r_ref[...]

    half = a_ref.shape[0] // 2
    for r in range(2):
        rows = slice(r * half, (r + 1) * half)
        o_ref[rows, :] += jnp.dot(a_ref[rows, :], w_ref[...], preferred_element_type=F32)


def _matmul_kacc(a, w_stack, layer, res, *, bm, bn, bk, name):
    m, kdim = a.shape
    n = w_stack.shape[2]
    blocks = [_nbytes((bm, bk), a.dtype), _nbytes((bk, bn), w_stack.dtype), 2 * _nbytes((bm, bn), F32)]
    return pl.pallas_call(
        _mm_kacc_body,
        grid=(m // bm, n // bn, kdim // bk),
        in_specs=[pl.BlockSpec((bm, bk), lambda i, j, k: (i, k)),
                  pl.BlockSpec((None, bk, bn), lambda i, j, k: (layer, k, j)),
                  pl.BlockSpec((bm, bn), lambda i, j, k: (i, j))],
        out_specs=pl.BlockSpec((bm, bn), lambda i, j, k: (i, j)),
        out_shape=jax.ShapeDtypeStruct((m, n), F32),
        compiler_params=pltpu.CompilerParams(
            dimension_semantics=("parallel", "parallel", "arbitrary"),
            vmem_limit_bytes=_vmem_limit(blocks, temp_bytes=2 * _nbytes((bm, bn), F32))),
        name=name,
    )(a, w_stack, res)


def _mm_wstat_body(*refs, n_extra, epilogue, n_col_blocks):
    a_ref, wn_ref = refs[0], refs[1]
    extra_refs = refs[2:2 + n_extra]
    o_ref = refs[2 + n_extra]
    wb_ref = refs[3 + n_extra]
    p = pl.program_id(0)
    i = pl.program_id(1)
    fill = p % 2
    kc = wn_ref.shape[0]

    @pl.when(p < n_col_blocks)
    def _():
        wb_ref[fill, pl.ds(pl.multiple_of(i * kc, kc), kc), :] = wn_ref[...].astype(BF16)

    @pl.when(p > 0)
    def _():
        acc = jnp.dot(a_ref[...], wb_ref[1 - fill], preferred_element_type=F32)
        o_ref[...] = epilogue(acc, *[r[...] for r in extra_refs]).astype(o_ref.dtype)


def _matmul_wstat(a, w_stack, layer, *, bm, bn, out_dtype, epilogue=_ep_identity, extras=(), name):
    m, kdim = a.shape
    n = w_stack.shape[2]
    n_row_blocks = m // bm
    n_col_blocks = n // bn
    kc = kdim // n_row_blocks
    row = lambda p, i: jnp.where(p == 0, 0, i)
    col = lambda p: jnp.maximum(p - 1, 0)
    in_specs = [pl.BlockSpec((bm, kdim), lambda p, i: (row(p, i), 0)),
                pl.BlockSpec((None, kc, bn), lambda p, i: (layer, i, jnp.minimum(p, n_col_blocks - 1)))]
    blocks = [_nbytes((bm, kdim), a.dtype), _nbytes((kc, bn), F32), _nbytes((bm, bn), out_dtype)]
    for arr, off in extras:
        in_specs.append(pl.BlockSpec((bm, bn), lambda p, i, off=off: (row(p, i), off + col(p))))
        blocks.append(_nbytes((bm, bn), arr.dtype))
    return pl.pallas_call(
        functools.partial(_mm_wstat_body, n_extra=len(extras), epilogue=epilogue, n_col_blocks=n_col_blocks),
        grid=(n_col_blocks + 1, n_row_blocks),
        in_specs=in_specs,
        out_specs=pl.BlockSpec((bm, bn), lambda p, i: (row(p, i), col(p))),
        out_shape=jax.ShapeDtypeStruct((m, n), out_dtype),
        scratch_shapes=[pltpu.VMEM((2, kdim, bn), BF16)],
        compiler_params=pltpu.CompilerParams(
            dimension_semantics=("arbitrary", "arbitrary"),
            vmem_limit_bytes=_vmem_limit(blocks, scratch_bytes=2 * _nbytes((kdim, bn), BF16),
                                         temp_bytes=3 * _nbytes((bm, bn), F32))),
        name=name,
    )(a, w_stack, *[arr for arr, _ in extras])


def _rope_body(ang_ref, cmask_ref, lo_sign_ref, hi_sign_ref, cos_ref, sin_lo_ref, sin_hi_ref):
    ang = ang_ref[...]
    sin = jnp.sin(ang)
    cos_ref[...] = jnp.cos(ang) * cmask_ref[...]
    sin_lo_ref[...] = sin * lo_sign_ref[...]
    sin_hi_ref[...] = sin * hi_sign_ref[...]


def _rope_tables(ang128, *, tm=1024):
    m = ang128.shape[0]
    lane = jnp.arange(V7X_LANES)
    lo = (lane < ROPE_HALF).astype(F32).reshape(1, V7X_LANES)
    hi = jnp.logical_and(lane >= ROPE_HALF, lane < A_ROPE).astype(F32).reshape(1, V7X_LANES)
    row = pl.BlockSpec((tm, V7X_LANES), lambda i: (i, 0))
    vec = pl.BlockSpec((1, V7X_LANES), lambda i: (0, 0))
    return pl.pallas_call(
        _rope_body,
        grid=(m // tm,),
        in_specs=[row, vec, vec, vec],
        out_specs=[row, row, row],
        out_shape=[jax.ShapeDtypeStruct((m, V7X_LANES), F32)] * 3,
        compiler_params=pltpu.CompilerParams(dimension_semantics=("parallel",)),
        name="rope_tables",
    )(ang128, lo + hi, -lo, hi)


def _rotate(x, rope):
    cos, sin_lo, sin_hi = rope
    hi_on_lo = pltpu.roll(x, shift=V7X_LANES - ROPE_HALF, axis=1)
    lo_on_hi = pltpu.roll(x, shift=ROPE_HALF, axis=1)
    return x * cos + hi_on_lo * sin_lo + lo_on_hi * sin_hi


def _split_bf16(x):
    hi = x.astype(BF16)
    r1 = x - hi.astype(F32)
    mid = r1.astype(BF16)
    lo = (r1 - mid.astype(F32)).astype(BF16)
    return hi, mid, lo


def _gate_body(i_ref, f_ref, bi_ref, bf_ref, bcol_ref, gcol_ref, grow_ref, *, chunk):
    li = GATE_SOFTCAP * jnp.tanh((i_ref[...] + bi_ref[...]) / GATE_SOFTCAP)
    z = GATE_SOFTCAP * jnp.tanh((f_ref[...] + bf_ref[...]) / GATE_SOFTCAP)
    lf = jnp.minimum(z, 0.0) - jnp.log(1.0 + jnp.exp(-jnp.abs(z)))
    row = lax.broadcasted_iota(jnp.int32, (chunk, chunk), 0)
    col = lax.broadcasted_iota(jnp.int32, (chunk, chunk), 1)
    tril = (col <= row).astype(BF16)
    b = sum(jnp.dot(tril, piece, preferred_element_type=F32) for piece in _split_bf16(lf))
    g = li - b
    bcol_ref[...] = b
    gcol_ref[...] = g
    grow_ref[...] = g.T[:V7X_SUBLANES, :]


def _mlstm_gates(small, b_i, b_f, *, i_blk, f_blk, chunk):
    m = small.shape[0]
    pad = lambda b: jnp.pad(b.astype(F32), (0, V7X_LANES - b.shape[0])).reshape(1, V7X_LANES)
    col = pl.BlockSpec((chunk, V7X_LANES), lambda c: (c, 0))
    vec = pl.BlockSpec((1, V7X_LANES), lambda c: (0, 0))
    return pl.pallas_call(
        functools.partial(_gate_body, chunk=chunk),
        grid=(m // chunk,),
        in_specs=[pl.BlockSpec((chunk, V7X_LANES), lambda c: (c, i_blk)),
                  pl.BlockSpec((chunk, V7X_LANES), lambda c: (c, f_blk)),
                  vec, vec],
        out_specs=[col, col, pl.BlockSpec((V7X_SUBLANES, chunk), lambda c: (0, c))],
        out_shape=[jax.ShapeDtypeStruct((m, V7X_LANES), F32),
                   jax.ShapeDtypeStruct((m, V7X_LANES), F32),
                   jax.ShapeDtypeStruct((V7X_SUBLANES, m), F32)],
        compiler_params=pltpu.CompilerParams(dimension_semantics=("parallel",)),
        name="mlstm_gates",
    )(small, small, pad(b_i), pad(b_f))


MLSTM_HEADS = 2


def _mlstm_head(head, q, k, v, o_gate, gain, bcol_all, gcol_all, grow_all, c_ref, n_ref, m_ref, *, chunk):
    lane_sel = lax.broadcasted_iota(jnp.int32, (1, V7X_LANES), 1) == head
    bcol = jnp.sum(jnp.where(lane_sel, bcol_all, 0.0), axis=-1, keepdims=True)
    gcol = jnp.sum(jnp.where(lane_sel, gcol_all, 0.0), axis=-1, keepdims=True)
    sub_sel = lax.broadcasted_iota(jnp.int32, (V7X_SUBLANES, 1), 0) == head
    grow = jnp.sum(jnp.where(sub_sel, grow_all, 0.0), axis=0, keepdims=True)

    m_prev = m_ref[...]
    row = lax.broadcasted_iota(jnp.int32, (chunk, chunk), 0)
    col = lax.broadcasted_iota(jnp.int32, (chunk, chunk), 1)
    dmat = jnp.where(col <= row, bcol + grow, -jnp.inf)
    inter = bcol + m_prev
    mj = jnp.maximum(inter, jnp.max(dmat, axis=-1, keepdims=True))
    w_inter = jnp.exp(inter - mj)

    scale = M_DQK ** -0.5
    qk = lax.dot_general(q, k, (((1,), (1,)), ((), ())), preferred_element_type=F32) * scale
    p = jnp.exp(dmat - mj) * qk
    cq = jnp.dot(q, c_ref[...].astype(BF16), preferred_element_type=F32) * scale
    num = w_inter * cq + jnp.dot(p.astype(BF16), v, preferred_element_type=F32)
    qn = jnp.sum(q.astype(F32) * n_ref[...], axis=-1, keepdims=True) * scale
    nq = w_inter * qn + jnp.sum(p, axis=-1, keepdims=True)
    hv = num / jnp.maximum(jnp.abs(nq), jnp.exp(-mj))

    ms = jnp.mean(hv * hv, axis=-1, keepdims=True)
    out = hv * lax.rsqrt(ms + EPS) * gain * jax.nn.sigmoid(o_gate.astype(F32))

    b_last = bcol[chunk - 1:chunk, :]
    acol = b_last + gcol
    m_new = jnp.maximum(b_last + m_prev, jnp.max(acol, axis=0, keepdims=True))
    decay = jnp.exp(b_last + m_prev - m_new)
    kw = k.astype(F32) * jnp.exp(acol - m_new)
    c_ref[...] = decay * c_ref[...] + lax.dot_general(
        kw.astype(BF16), v, (((0,), (0,)), ((), ())), preferred_element_type=F32)
    n_ref[...] = decay * n_ref[...] + jnp.sum(kw, axis=0, keepdims=True)
    m_ref[...] = m_new
    return out


def _mlstm_body(q_ref, k_ref, v_ref, o_ref, bcol_ref, gcol_ref, grow_ref, hn_ref, out_ref,
                c_ref, n_ref, m_ref, *, chunk):
    @pl.when(pl.program_id(2) == 0)
    def _():
        c_ref[...] = jnp.zeros_like(c_ref)
        n_ref[...] = jnp.zeros_like(n_ref)
        m_ref[...] = jnp.zeros_like(m_ref)

    for e in range(MLSTM_HEADS):
        qk_cols = slice(e * M_DQK, (e + 1) * M_DQK)
        v_cols = slice(e * M_DV, (e + 1) * M_DV)
        out = _mlstm_head(pl.program_id(1) * MLSTM_HEADS + e, q_ref[:, qk_cols], k_ref[:, qk_cols],
                          v_ref[:, v_cols], o_ref[:, v_cols], hn_ref[:, v_cols],
                          bcol_ref[...], gcol_ref[...], grow_ref[...],
                          c_ref.at[e], n_ref.at[e], m_ref.at[e], chunk=chunk)
        out_ref[:, v_cols] = out.astype(out_ref.dtype)


def _mlstm(proj, bcol, gcol, grow, head_gain, *, batch, seq, chunk):
    m = proj.shape[0]
    nc = seq // chunk
    hb = MLSTM_HEADS
    k_blk0 = M_HEADS // hb
    v_blk0 = (2 * M_HEADS * M_DQK) // (hb * M_DV)
    o_blk0 = v_blk0 + M_HEADS // hb
    rows = lambda b, h, c: b * nc + c
    return pl.pallas_call(
        functools.partial(_mlstm_body, chunk=chunk),
        grid=(batch, M_HEADS // hb, nc),
        in_specs=[
            pl.BlockSpec((chunk, hb * M_DQK), lambda b, h, c: (rows(b, h, c), h)),
            pl.BlockSpec((chunk, hb * M_DQK), lambda b, h, c: (rows(b, h, c), k_blk0 + h)),
            pl.BlockSpec((chunk, hb * M_DV), lambda b, h, c: (rows(b, h, c), v_blk0 + h)),
            pl.BlockSpec((chunk, hb * M_DV), lambda b, h, c: (rows(b, h, c), o_blk0 + h)),
            pl.BlockSpec((chunk, V7X_LANES), lambda b, h, c: (rows(b, h, c), 0)),
            pl.BlockSpec((chunk, V7X_LANES), lambda b, h, c: (rows(b, h, c), 0)),
            pl.BlockSpec((V7X_SUBLANES, chunk), lambda b, h, c: (0, rows(b, h, c))),
            pl.BlockSpec((1, hb * M_DV), lambda b, h, c: (0, h)),
        ],
        out_specs=pl.BlockSpec((chunk, hb * M_DV), lambda b, h, c: (rows(b, h, c), h)),
        out_shape=jax.ShapeDtypeStruct((m, M_HEADS * M_DV), BF16),
        scratch_shapes=[pltpu.VMEM((hb, M_DQK, M_DV), F32),
                        pltpu.VMEM((hb, 1, M_DQK), F32),
                        pltpu.VMEM((hb, 1, 1), F32)],
        compiler_params=pltpu.CompilerParams(
            dimension_semantics=("parallel", "parallel", "arbitrary")),
        name="mlstm",
    )(proj, proj, proj, proj, bcol, gcol, grow, head_gain.reshape(1, M_HEADS * M_DV))


def _row_rmsnorm(a, gain):
    ms = jnp.mean(a * a, axis=-1, keepdims=True)
    return (a * lax.rsqrt(ms + EPS) * gain).astype(BF16)


def _qproj_body(a_ref, an_ref, w_ref, gq_ref, cos_ref, sin_lo_ref, sin_hi_ref, o_ref, *, heads_per_block):
    a = _row_rmsnorm(a_ref[...], an_ref[...])
    cos = cos_ref[...]
    sin = sin_lo_ref[...] + sin_hi_ref[...]
    gq = gq_ref[...]
    real = lax.broadcasted_iota(jnp.int32, (1, A_HEAD_PAD), 1) < A_QK
    half = a.shape[0] // 2
    for r in range(2):
        rows = slice(r * half, (r + 1) * half)
        acc = jnp.dot(a[rows], w_ref[...], preferred_element_type=F32)
        for hh in range(heads_per_block):
            blk = acc[:, hh * A_HEAD_PAD:(hh + 1) * A_HEAD_PAD]
            ss = jnp.sum(jnp.where(real, blk * blk, 0.0), axis=-1, keepdims=True)
            qn = blk * lax.rsqrt(ss * (1.0 / A_QK) + EPS) * gq
            rope = qn[:, V7X_LANES:]
            rope = rope * cos[rows] + pltpu.roll(rope, shift=V7X_LANES - ROPE_HALF, axis=1) * sin[rows]
            o_ref[rows, hh * A_HEAD_PAD:hh * A_HEAD_PAD + V7X_LANES] = qn[:, :V7X_LANES].astype(o_ref.dtype)
            o_ref[rows, hh * A_HEAD_PAD + V7X_LANES:(hh + 1) * A_HEAD_PAD] = rope.astype(o_ref.dtype)


def _qproj(small, a_norm, w_q, gq, rope, *, rank, bm=1024, heads_per_block=4):
    m = small.shape[0]
    bn = heads_per_block * A_HEAD_PAD
    n = w_q.shape[1]
    rowvec = pl.BlockSpec((bm, V7X_LANES), lambda i, j: (i, 0))
    return pl.pallas_call(
        functools.partial(_qproj_body, heads_per_block=heads_per_block),
        grid=(m // bm, n // bn),
        in_specs=[pl.BlockSpec((bm, rank), lambda i, j: (i, 0)),
                  pl.BlockSpec((1, rank), lambda i, j: (0, 0)),
                  pl.BlockSpec((rank, bn), lambda i, j: (0, j)),
                  pl.BlockSpec((1, A_HEAD_PAD), lambda i, j: (0, 0)),
                  rowvec, rowvec, rowvec],
        out_specs=pl.BlockSpec((bm, bn), lambda i, j: (i, j)),
        out_shape=jax.ShapeDtypeStruct((m, n), BF16),
        compiler_params=pltpu.CompilerParams(
            dimension_semantics=("parallel", "parallel"),
            vmem_limit_bytes=_vmem_limit([_nbytes((bm, rank), F32), _nbytes((rank, bn), BF16),
                                          _nbytes((bm, bn), BF16)],
                                         temp_bytes=4 * _nbytes((bm, bn), F32))),
        name="mla_qproj",
    )(small, a_norm.reshape(1, rank), w_q, gq, *rope)


def _kvproj_body(a_ref, an_ref, w_ref, kr_ref, gkn_ref, gkr_ref, cos_ref, sin_lo_ref, sin_hi_ref, k_ref, v_ref, *,
                 heads_per_block):
    a = _row_rmsnorm(a_ref[...], an_ref[...])
    kr = kr_ref[...]
    ss_rope = jnp.sum(kr * kr, axis=-1, keepdims=True)
    k_rot = _rotate(kr * gkr_ref[...], (cos_ref[...], sin_lo_ref[...], sin_hi_ref[...]))
    gkn = gkn_ref[...]
    half = a.shape[0] // 2
    ones = jnp.ones((half, A_DV), v_ref.dtype)
    for r in range(2):
        rows = slice(r * half, (r + 1) * half)
        acc = jnp.dot(a[rows], w_ref[...], preferred_element_type=F32)
        for hh in range(heads_per_block):
            kv = acc[:, hh * (A_NOPE + A_DV):(hh + 1) * (A_NOPE + A_DV)]
            kn = kv[:, :A_NOPE]
            ss = jnp.sum(kn * kn, axis=-1, keepdims=True) + ss_rope[rows]
            rs = lax.rsqrt(ss * (1.0 / A_QK) + EPS)
            k_ref[rows, hh * A_HEAD_PAD:hh * A_HEAD_PAD + V7X_LANES] = (kn * rs * gkn).astype(k_ref.dtype)
            k_ref[rows, hh * A_HEAD_PAD + V7X_LANES:(hh + 1) * A_HEAD_PAD] = (k_rot[rows] * rs).astype(k_ref.dtype)
            v_ref[rows, hh * A_V_PAD:hh * A_V_PAD + A_DV] = kv[:, A_NOPE:].astype(v_ref.dtype)
            v_ref[rows, hh * A_V_PAD + A_DV:(hh + 1) * A_V_PAD] = ones


def _kvproj(small, a_norm, w_kv, gkn, gkr, rope, *, a_blk, rank, rope_blk, bm=1024, heads_per_block=4):
    m = small.shape[0]
    bn = heads_per_block * (A_NOPE + A_DV)
    n_blocks = w_kv.shape[1] // bn
    rowvec = pl.BlockSpec((bm, V7X_LANES), lambda i, j: (i, 0))
    lanevec = pl.BlockSpec((1, V7X_LANES), lambda i, j: (0, 0))
    return pl.pallas_call(
        functools.partial(_kvproj_body, heads_per_block=heads_per_block),
        grid=(m // bm, n_blocks),
        in_specs=[pl.BlockSpec((bm, rank), lambda i, j: (i, a_blk)),
                  pl.BlockSpec((1, rank), lambda i, j: (0, 0)),
                  pl.BlockSpec((rank, bn), lambda i, j: (0, j)),
                  pl.BlockSpec((bm, V7X_LANES), lambda i, j: (i, rope_blk)),
                  lanevec, lanevec, rowvec, rowvec, rowvec],
        out_specs=[pl.BlockSpec((bm, heads_per_block * A_HEAD_PAD), lambda i, j: (i, j)),
                   pl.BlockSpec((bm, heads_per_block * A_V_PAD), lambda i, j: (i, j))],
        out_shape=[jax.ShapeDtypeStruct((m, A_HEADS * A_HEAD_PAD), BF16),
                   jax.ShapeDtypeStruct((m, A_HEADS * A_V_PAD), BF16)],
        compiler_params=pltpu.CompilerParams(
            dimension_semantics=("parallel", "parallel"),
            vmem_limit_bytes=_vmem_limit([_nbytes((bm, rank), F32), _nbytes((rank, bn), BF16),
                                          _nbytes((bm, bn), BF16), _nbytes((bm, bn), BF16)],
                                         temp_bytes=4 * _nbytes((bm, bn), F32))),
        name="mla_kvproj",
    )(small, a_norm.reshape(1, rank), w_kv, small, gkn, gkr, *rope)


FLASH_HEADS = 2


def _flash_body(q_ref, k_ref, v_ref, o_ref, s_ref, mx_ref, acc_ref, *, tq, tkc):
    i = pl.program_id(2)
    n_slab = tkc // V7X_LANES
    n_unmasked = (i * tq) // tkc
    n_chunks = ((i + 1) * tq + tkc - 1) // tkc
    mx_ref[...] = jnp.full_like(mx_ref, -jnp.inf)
    acc_ref[...] = jnp.zeros_like(acc_ref)

    def scores(c, masked):
        rows = pl.ds(pl.multiple_of(c * tkc, tkc), tkc)
        for e in range(FLASH_HEADS):
            s = lax.dot_general(q_ref[:, e * A_HEAD_PAD:(e + 1) * A_HEAD_PAD],
                                k_ref[rows, e * A_HEAD_PAD:(e + 1) * A_HEAD_PAD],
                                (((1,), (1,)), ((), ())), preferred_element_type=F32)
            if masked:
                row = lax.broadcasted_iota(jnp.int32, (tq, tkc), 0)
                col = lax.broadcasted_iota(jnp.int32, (tq, tkc), 1)
                s = jnp.where(col + c * tkc <= row + i * tq, s, -jnp.inf)
            s_ref[e, c] = s
            mx = mx_ref[e]
            for t in range(n_slab):
                mx = jnp.maximum(mx, s[:, t * V7X_LANES:(t + 1) * V7X_LANES])
            mx_ref[e] = mx

    def unmasked_chunk(c, carry):
        scores(c, False)
        return carry

    def masked_chunk(c, carry):
        scores(c, True)
        return carry

    lax.fori_loop(0, n_unmasked, unmasked_chunk, 0)
    lax.fori_loop(n_unmasked, n_chunks, masked_chunk, 0)

    for e in range(FLASH_HEADS):
        mx_ref[e] = jnp.broadcast_to(jnp.max(mx_ref[e], axis=-1, keepdims=True), (tq, V7X_LANES))

    def probs_chunk(c, carry):
        rows = pl.ds(pl.multiple_of(c * tkc, tkc), tkc)
        for e in range(FLASH_HEADS):
            s = s_ref[e, c]
            mx = mx_ref[e]
            p = jnp.concatenate([jnp.exp2(s[:, t * V7X_LANES:(t + 1) * V7X_LANES] - mx) for t in range(n_slab)],
                                axis=-1).astype(BF16)
            acc_ref[e] += jnp.dot(p, v_ref[rows, e * A_V_PAD:(e + 1) * A_V_PAD], preferred_element_type=F32)
        return carry

    lax.fori_loop(0, n_chunks, probs_chunk, 0)
    for e in range(FLASH_HEADS):
        acc = acc_ref[e]
        o_ref[:, e * A_DV:(e + 1) * A_DV] = (acc[:, :A_DV] / acc[:, A_DV:]).astype(o_ref.dtype)


def _flash_attention(q, k, v1, *, batch, seq, tq=512, tkc=1024):
    m = q.shape[0]
    nq = seq // tq
    hb = FLASH_HEADS
    blocks = [_nbytes((tq, hb * A_HEAD_PAD), BF16), _nbytes((seq, hb * A_HEAD_PAD), BF16),
              _nbytes((seq, hb * A_V_PAD), BF16), _nbytes((tq, hb * A_DV), BF16)]
    scratch_shapes = [pltpu.VMEM((hb, seq // tkc, tq, tkc), F32),
                      pltpu.VMEM((hb, tq, V7X_LANES), F32),
                      pltpu.VMEM((hb, tq, A_V_PAD), F32)]
    scratch = _nbytes((hb, tq, seq), F32) + _nbytes((hb, tq, V7X_LANES), F32) + _nbytes((hb, tq, A_V_PAD), F32)
    return pl.pallas_call(
        functools.partial(_flash_body, tq=tq, tkc=tkc),
        grid=(batch, A_HEADS // hb, nq),
        in_specs=[
            pl.BlockSpec((tq, hb * A_HEAD_PAD), lambda b, h, i: (b * nq + i, h)),
            pl.BlockSpec((seq, hb * A_HEAD_PAD), lambda b, h, i: (b, h)),
            pl.BlockSpec((seq, hb * A_V_PAD), lambda b, h, i: (b, h)),
        ],
        out_specs=pl.BlockSpec((tq, hb * A_DV), lambda b, h, i: (b * nq + i, h)),
        out_shape=jax.ShapeDtypeStruct((m, A_HEADS * A_DV), BF16),
        scratch_shapes=scratch_shapes,
        compiler_params=pltpu.CompilerParams(
            dimension_semantics=("parallel", "parallel", "parallel"),
            vmem_limit_bytes=_vmem_limit(blocks, scratch_bytes=scratch,
                                         temp_bytes=hb * 3 * _nbytes((tq, tkc), F32))),
        name="mla_flash",
    )(q, k, v1)


def _pad_lanes(x, width):
    return jnp.pad(x, [(0, 0)] * (x.ndim - 1) + [(0, width - x.shape[-1])])


def _layer(xf, rope, layer, norm_mix, w_in_bf16_stack, b_igate, b_fgate, m_head_norm, w_out_m_stack, q_a_norm,
           w_uq, kv_a_norm, w_ukv, qk_norm_q, qk_norm_k, w_out_a_stack, w_out_stack, norm_mlp, w_up_stack,
           w_down_bf16_stack, *, batch, seq):
    d_model = xf.shape[1]
    q_lora = w_uq.shape[0]
    kv_lora = w_ukv.shape[0]
    m_qk_w = M_HEADS * M_DQK
    m_v_w = M_HEADS * M_DV
    w_in_cols = lambda lo, hi: lax.slice(w_in_bf16_stack, (layer, 0, lo), (layer + 1, d_model, hi))[0]
    o_i = 2 * m_qk_w + 2 * m_v_w
    o_f = o_i + M_HEADS
    o_cq = o_f + M_HEADS
    o_kva = o_cq + q_lora
    o_ga = o_kva + kv_lora + A_ROPE
    w_gates = w_in_cols(o_ga, o_ga + 2 * d_model)
    w_small = jnp.concatenate([
        _pad_lanes(w_in_cols(o_cq, o_ga), q_lora + kv_lora + V7X_LANES),
        _pad_lanes(w_in_cols(o_i, o_f), V7X_LANES),
        _pad_lanes(w_in_cols(o_f, o_cq), V7X_LANES),
    ], axis=1)
    kv_blk = q_lora // kv_lora
    rope_blk = (q_lora + kv_lora) // V7X_LANES
    i_blk = rope_blk + 1
    f_blk = i_blk + 1

    twice_rope = lambda t: jnp.concatenate([t, t[..., A_NOPE:]], axis=-1)
    w_q = twice_rope(w_uq.astype(BF16).reshape(q_lora, A_HEADS, A_QK)).reshape(q_lora, A_HEADS * A_HEAD_PAD)
    gq = twice_rope(qk_norm_q * (A_QK ** -0.5 * LOG2E)).reshape(1, A_HEAD_PAD)
    hpb = 4
    w_kv = w_ukv.astype(BF16)
    gkn = qk_norm_k[:A_NOPE].reshape(1, V7X_LANES)
    gkr = _pad_lanes(qk_norm_k[A_NOPE:], V7X_LANES).reshape(1, V7X_LANES)

    h = _rmsnorm(xf, norm_mix)
    proj = _matmul(h, w_in_bf16_stack, layer=layer, n_cols=o_i, bm=1024, bn=1024, out_dtype=BF16,
                   name="in_proj_main")
    gab = _matmul(h, w_gates, bm=1024, bn=1024, out_dtype=BF16, name="in_proj_gates")
    small = _matmul(h, w_small, bm=1024, bn=w_small.shape[1] // 3, out_dtype=F32, name="in_proj_small")

    bcol, gcol, grow = _mlstm_gates(small, b_igate, b_fgate, i_blk=i_blk, f_blk=f_blk, chunk=MLSTM_CHUNK)
    hm = _mlstm(proj, bcol, gcol, grow, m_head_norm, batch=batch, seq=seq, chunk=MLSTM_CHUNK)

    qh = _qproj(small, q_a_norm, w_q, gq, rope, rank=q_lora, heads_per_block=hpb)
    kh, vh = _kvproj(small, kv_a_norm, w_kv, gkn, gkr, rope, a_blk=kv_blk, rank=kv_lora,
                     rope_blk=rope_blk, heads_per_block=hpb)
    att = _flash_attention(qh, kh, vh, batch=batch, seq=seq)

    bn = 1024
    t = _matmul_wstat(hm, w_out_m_stack, layer, bm=512, bn=bn, out_dtype=F32,
                      epilogue=_ep_gate, extras=[(gab, 0)], name="out_proj_mlstm")
    merged = _matmul_wstat(att, w_out_a_stack, layer, bm=512, bn=bn, out_dtype=BF16,
                           epilogue=_ep_gate_add, extras=[(gab, d_model // bn), (t, 0)], name="out_proj_mla")
    xf = _matmul_wstat(merged, w_out_stack, layer, bm=512, bn=bn, out_dtype=F32,
                       epilogue=_ep_residual, extras=[(xf, 0)], name="out_proj")

    h2 = _rmsnorm(xf, norm_mlp)
    u = _matmul_wstat(h2, w_up_stack, layer, bm=1024, bn=1024, out_dtype=BF16, epilogue=_ep_relu2, name="mlp_up")
    return _matmul_kacc(u, w_down_bf16_stack, layer, xf, bm=1024, bn=1024, bk=4096, name="mlp_down")


def kernel(x, positions, norm_mix, w_in, b_igate, b_fgate, m_head_norm, w_out_m, q_a_norm, w_uq,
           kv_a_norm, w_ukv, qk_norm_q, qk_norm_k, w_out_a, w_out, norm_mlp, w_up, w_down):
    batch, seq, d_model = x.shape
    depth = w_in.shape[0]
    m = batch * seq
    inv_freq = ROPE_THETA ** (-jnp.arange(0, A_ROPE, 2, dtype=F32) / A_ROPE)
    ang = positions.astype(F32).reshape(m, 1) * inv_freq
    rope = _rope_tables(_pad_lanes(jnp.concatenate([ang, ang], axis=-1), V7X_LANES))
    xf = x.reshape(m, d_model)
    w_in_bf16 = w_in.astype(BF16)
    w_down_bf16 = w_down.astype(BF16)
    for l in range(depth):
        xf = _layer(xf, rope, l, norm_mix[l], w_in_bf16, b_igate[l], b_fgate[l], m_head_norm[l], w_out_m,
                    q_a_norm[l], w_uq[l], kv_a_norm[l], w_ukv[l], qk_norm_q[l], qk_norm_k[l], w_out_a,
                    w_out, norm_mlp[l], w_up, w_down_bf16, batch=batch, seq=seq)
    return xf.reshape(batch, seq, d_model)
```

```python
import functools
import math

import jax
import jax.numpy as jnp
from jax import lax
from jax.experimental import pallas as pl
from jax.experimental.pallas import tpu as pltpu

F32 = jnp.float32
BF16 = jnp.bfloat16

M_HEADS = 8
M_DQK = 256
M_DV = 512
GATE_SOFTCAP = 15.0
A_HEADS = 32
A_NOPE = 128
A_ROPE = 64
A_QK = A_NOPE + A_ROPE
A_DV = 128
ROPE_THETA = 10000.0
EPS = 1e-6

V7X_LANES = 128
V7X_SUBLANES = 8
V7X_VMEM_BYTES = 64 * 1024 * 1024
V7X_VMEM_REQUEST_CAP = V7X_VMEM_BYTES - 6 * 1024 * 1024

A_HEAD_PAD = 2 * V7X_LANES
A_V_PAD = 2 * A_DV
ROPE_HALF = A_ROPE // 2
MLSTM_CHUNK = 256
LOG2E = math.log2(math.e)


def _vmem_limit(block_bytes, scratch_bytes=0, temp_bytes=0):
    need = 2 * sum(block_bytes) + scratch_bytes + temp_bytes + (2 << 20)
    return int(min(max(need, 16 << 20), V7X_VMEM_REQUEST_CAP))


def _nbytes(shape, dtype):
    return math.prod(shape) * jnp.dtype(dtype).itemsize


def _rmsnorm_body(x_ref, g_ref, o_ref):
    x = x_ref[...]
    ms = jnp.mean(x * x, axis=-1, keepdims=True)
    o_ref[...] = (x * lax.rsqrt(ms + EPS) * g_ref[...]).astype(o_ref.dtype)


def _rmsnorm(x, g, *, tm=256):
    m, d = x.shape
    return pl.pallas_call(
        _rmsnorm_body,
        grid=(m // tm,),
        in_specs=[pl.BlockSpec((tm, d), lambda i: (i, 0)),
                  pl.BlockSpec((1, d), lambda i: (0, 0))],
        out_specs=pl.BlockSpec((tm, d), lambda i: (i, 0)),
        out_shape=jax.ShapeDtypeStruct((m, d), BF16),
        compiler_params=pltpu.CompilerParams(
            dimension_semantics=("parallel",),
            vmem_limit_bytes=_vmem_limit([_nbytes((tm, d), F32), _nbytes((tm, d), BF16)],
                                         temp_bytes=2 * _nbytes((tm, d), F32))),
        name="rmsnorm",
    )(x, g.reshape(1, d))


def _ep_identity(acc):
    return acc


def _ep_relu2(acc):
    return jnp.square(jnp.maximum(acc, 0.0))


def _ep_gate(acc, g):
    return jax.nn.sigmoid(g.astype(F32)) * acc


def _ep_gate_add(acc, g, t):
    return t + jax.nn.sigmoid(g.astype(F32)) * acc


def _ep_residual(acc, r):
    return r + acc


def _mm_body(*refs, n_extra, epilogue):
    a_ref, w_ref = refs[0], refs[1]
    extra_refs = refs[2:2 + n_extra]
    o_ref = refs[2 + n_extra]
    acc = jnp.dot(a_ref[...], w_ref[...], preferred_element_type=F32)
    o_ref[...] = epilogue(acc, *[r[...] for r in extra_refs]).astype(o_ref.dtype)


def _matmul(a, w, *, bm, bn, out_dtype, epilogue=_ep_identity, extras=(), layer=None, n_cols=None, name):
    m, kdim = a.shape
    if layer is None:
        n = w.shape[1]
        w_spec = pl.BlockSpec((kdim, bn), lambda i, j: (0, j))
    else:
        n = n_cols
        w_spec = pl.BlockSpec((None, kdim, bn), lambda i, j: (layer, 0, j))
    in_specs = [pl.BlockSpec((bm, kdim), lambda i, j: (i, 0)), w_spec]
    blocks = [_nbytes((bm, kdim), a.dtype), _nbytes((kdim, bn), w.dtype), _nbytes((bm, bn), out_dtype)]
    for arr, off in extras:
        in_specs.append(pl.BlockSpec((bm, bn), lambda i, j, off=off: (i, off + j)))
        blocks.append(_nbytes((bm, bn), arr.dtype))
    return pl.pallas_call(
        functools.partial(_mm_body, n_extra=len(extras), epilogue=epilogue),
        grid=(m // bm, n // bn),
        in_specs=in_specs,
        out_specs=pl.BlockSpec((bm, bn), lambda i, j: (i, j)),
        out_shape=jax.ShapeDtypeStruct((m, n), out_dtype),
        compiler_params=pltpu.CompilerParams(
            dimension_semantics=("parallel", "parallel"),
            vmem_limit_bytes=_vmem_limit(blocks, temp_bytes=3 * _nbytes((bm, bn), F32))),
        name=name,
    )(a, w, *[arr for arr, _ in extras])


def _mm_kacc_body(a_ref, w_ref, r_ref, o_ref):
    @pl.when(pl.program_id(2) == 0)
    def _():
        o_ref[...] = r_ref[...]

    half = a_ref.shape[0] // 2
    for r in range(2):
        rows = slice(r * half, (r + 1) * half)
        o_ref[rows, :] += jnp.dot(a_ref[rows, :], w_ref[...], preferred_element_type=F32)


def _matmul_kacc(a, w_stack, layer, res, *, bm, bn, bk, name):
    m, kdim = a.shape
    n = w_stack.shape[2]
    blocks = [_nbytes((bm, bk), a.dtype), _nbytes((bk, bn), w_stack.dtype), 2 * _nbytes((bm, bn), F32)]
    return pl.pallas_call(
        _mm_kacc_body,
        grid=(m // bm, n // bn, kdim // bk),
        in_specs=[pl.BlockSpec((bm, bk), lambda i, j, k: (i, k)),
                  pl.BlockSpec((None, bk, bn), lambda i, j, k: (layer, k, j)),
                  pl.BlockSpec((bm, bn), lambda i, j, k: (i, j))],
        out_specs=pl.BlockSpec((bm, bn), lambda i, j, k: (i, j)),
        out_shape=jax.ShapeDtypeStruct((m, n), F32),
        compiler_params=pltpu.CompilerParams(
            dimension_semantics=("parallel", "parallel", "arbitrary"),
            vmem_limit_bytes=_vmem_limit(blocks, temp_bytes=2 * _nbytes((bm, bn), F32))),
        name=name,
    )(a, w_stack, res)


def _mm_wstat_body(*refs, n_extra, epilogue, n_col_blocks):
    a_ref, wn_ref = refs[0], refs[1]
    extra_refs = refs[2:2 + n_extra]
    o_ref = refs[2 + n_extra]
    wb_ref = refs[3 + n_extra]
    p = pl.program_id(0)
    i = pl.program_id(1)
    fill = p % 2
    kc = wn_ref.shape[0]

    @pl.when(p < n_col_blocks)
    def _():
        wb_ref[fill, pl.ds(pl.multiple_of(i * kc, kc), kc), :] = wn_ref[...].astype(BF16)

    @pl.when(p > 0)
    def _():
        acc = jnp.dot(a_ref[...], wb_ref[1 - fill], preferred_element_type=F32)
        o_ref[...] = epilogue(acc, *[r[...] for r in extra_refs]).astype(o_ref.dtype)


def _matmul_wstat(a, w_stack, layer, *, bm, bn, out_dtype, epilogue=_ep_identity, extras=(), name):
    m, kdim = a.shape
    n = w_stack.shape[2]
    n_row_blocks = m // bm
    n_col_blocks = n // bn
    kc = kdim // n_row_blocks
    row = lambda p, i: jnp.where(p == 0, 0, i)
    col = lambda p: jnp.maximum(p - 1, 0)
    in_specs = [pl.BlockSpec((bm, kdim), lambda p, i: (row(p, i), 0)),
                pl.BlockSpec((None, kc, bn), lambda p, i: (layer, i, jnp.minimum(p, n_col_blocks - 1)))]
    blocks = [_nbytes((bm, kdim), a.dtype), _nbytes((kc, bn), F32), _nbytes((bm, bn), out_dtype)]
    for arr, off in extras:
        in_specs.append(pl.BlockSpec((bm, bn), lambda p, i, off=off: (row(p, i), off + col(p))))
        blocks.append(_nbytes((bm, bn), arr.dtype))
    return pl.pallas_call(
        functools.partial(_mm_wstat_body, n_extra=len(extras), epilogue=epilogue, n_col_blocks=n_col_blocks),
        grid=(n_col_blocks + 1, n_row_blocks),
        in_specs=in_specs,
        out_specs=pl.BlockSpec((bm, bn), lambda p, i: (row(p, i), col(p))),
        out_shape=jax.ShapeDtypeStruct((m, n), out_dtype),
        scratch_shapes=[pltpu.VMEM((2, kdim, bn), BF16)],
        compiler_params=pltpu.CompilerParams(
            dimension_semantics=("arbitrary", "arbitrary"),
            vmem_limit_bytes=_vmem_limit(blocks, scratch_bytes=2 * _nbytes((kdim, bn), BF16),
                                         temp_bytes=3 * _nbytes((bm, bn), F32))),
        name=name,
    )(a, w_stack, *[arr for arr, _ in extras])


def _rope_body(ang_ref, cmask_ref, lo_sign_ref, hi_sign_ref, cos_ref, sin_lo_ref, sin_hi_ref):
    ang = ang_ref[...]
    sin = jnp.sin(ang)
    cos_ref[...] = jnp.cos(ang) * cmask_ref[...]
    sin_lo_ref[...] = sin * lo_sign_ref[...]
    sin_hi_ref[...] = sin * hi_sign_ref[...]


def _rope_tables(ang128, *, tm=1024):
    m = ang128.shape[0]
    lane = jnp.arange(V7X_LANES)
    lo = (lane < ROPE_HALF).astype(F32).reshape(1, V7X_LANES)
    hi = jnp.logical_and(lane >= ROPE_HALF, lane < A_ROPE).astype(F32).reshape(1, V7X_LANES)
    row = pl.BlockSpec((tm, V7X_LANES), lambda i: (i, 0))
    vec = pl.BlockSpec((1, V7X_LANES), lambda i: (0, 0))
    return pl.pallas_call(
        _rope_body,
        grid=(m // tm,),
        in_specs=[row, vec, vec, vec],
        out_specs=[row, row, row],
        out_shape=[jax.ShapeDtypeStruct((m, V7X_LANES), F32)] * 3,
        compiler_params=pltpu.CompilerParams(dimension_semantics=("parallel",)),
        name="rope_tables",
    )(ang128, lo + hi, -lo, hi)


def _rotate(x, rope):
    cos, sin_lo, sin_hi = rope
    hi_on_lo = pltpu.roll(x, shift=V7X_LANES - ROPE_HALF, axis=1)
    lo_on_hi = pltpu.roll(x, shift=ROPE_HALF, axis=1)
    return x * cos + hi_on_lo * sin_lo + lo_on_hi * sin_hi


def _split_bf16(x):
    hi = x.astype(BF16)
    r1 = x - hi.astype(F32)
    mid = r1.astype(BF16)
    lo = (r1 - mid.astype(F32)).astype(BF16)
    return hi, mid, lo


def _gate_body(i_ref, f_ref, bi_ref, bf_ref, bcol_ref, gcol_ref, grow_ref, *, chunk):
    li = GATE_SOFTCAP * jnp.tanh((i_ref[...] + bi_ref[...]) / GATE_SOFTCAP)
    z = GATE_SOFTCAP * jnp.tanh((f_ref[...] + bf_ref[...]) / GATE_SOFTCAP)
    lf = jnp.minimum(z, 0.0) - jnp.log(1.0 + jnp.exp(-jnp.abs(z)))
    row = lax.broadcasted_iota(jnp.int32, (chunk, chunk), 0)
    col = lax.broadcasted_iota(jnp.int32, (chunk, chunk), 1)
    tril = (col <= row).astype(BF16)
    b = sum(jnp.dot(tril, piece, preferred_element_type=F32) for piece in _split_bf16(lf))
    g = li - b
    bcol_ref[...] = b
    gcol_ref[...] = g
    grow_ref[...] = g.T[:V7X_SUBLANES, :]


def _mlstm_gates(small, b_i, b_f, *, i_blk, f_blk, chunk):
    m = small.shape[0]
    pad = lambda b: jnp.pad(b.astype(F32), (0, V7X_LANES - b.shape[0])).reshape(1, V7X_LANES)
    col = pl.BlockSpec((chunk, V7X_LANES), lambda c: (c, 0))
    vec = pl.BlockSpec((1, V7X_LANES), lambda c: (0, 0))
    return pl.pallas_call(
        functools.partial(_gate_body, chunk=chunk),
        grid=(m // chunk,),
        in_specs=[pl.BlockSpec((chunk, V7X_LANES), lambda c: (c, i_blk)),
                  pl.BlockSpec((chunk, V7X_LANES), lambda c: (c, f_blk)),
                  vec, vec],
        out_specs=[col, col, pl.BlockSpec((V7X_SUBLANES, chunk), lambda c: (0, c))],
        out_shape=[jax.ShapeDtypeStruct((m, V7X_LANES), F32),
                   jax.ShapeDtypeStruct((m, V7X_LANES), F32),
                   jax.ShapeDtypeStruct((V7X_SUBLANES, m), F32)],
        compiler_params=pltpu.CompilerParams(dimension_semantics=("parallel",)),
        name="mlstm_gates",
    )(small, small, pad(b_i), pad(b_f))


MLSTM_HEADS = 2


def _mlstm_head(head, q, k, v, o_gate, gain, bcol_all, gcol_all, grow_all, c_ref, n_ref, m_ref, *, chunk):
    lane_sel = lax.broadcasted_iota(jnp.int32, (1, V7X_LANES), 1) == head
    bcol = jnp.sum(jnp.where(lane_sel, bcol_all, 0.0), axis=-1, keepdims=True)
    gcol = jnp.sum(jnp.where(lane_sel, gcol_all, 0.0), axis=-1, keepdims=True)
    sub_sel = lax.broadcasted_iota(jnp.int32, (V7X_SUBLANES, 1), 0) == head
    grow = jnp.sum(jnp.where(sub_sel, grow_all, 0.0), axis=0, keepdims=True)

    m_prev = m_ref[...]
    row = lax.broadcasted_iota(jnp.int32, (chunk, chunk), 0)
    col = lax.broadcasted_iota(jnp.int32, (chunk, chunk), 1)
    dmat = jnp.where(col <= row, bcol + grow, -jnp.inf)
    inter = bcol + m_prev
    mj = jnp.maximum(inter, jnp.max(dmat, axis=-1, keepdims=True))
    w_inter = jnp.exp(inter - mj)

    scale = M_DQK ** -0.5
    qk = lax.dot_general(q, k, (((1,), (1,)), ((), ())), preferred_element_type=F32) * scale
    p = jnp.exp(dmat - mj) * qk
    cq = jnp.dot(q, c_ref[...].astype(BF16), preferred_element_type=F32) * scale
    num = w_inter * cq + jnp.dot(p.astype(BF16), v, preferred_element_type=F32)
    qn = jnp.sum(q.astype(F32) * n_ref[...], axis=-1, keepdims=True) * scale
    nq = w_inter * qn + jnp.sum(p, axis=-1, keepdims=True)
    hv = num / jnp.maximum(jnp.abs(nq), jnp.exp(-mj))

    ms = jnp.mean(hv * hv, axis=-1, keepdims=True)
    out = hv * lax.rsqrt(ms + EPS) * gain * jax.nn.sigmoid(o_gate.astype(F32))

    b_last = bcol[chunk - 1:chunk, :]
    acol = b_last + gcol
    m_new = jnp.maximum(b_last + m_prev, jnp.max(acol, axis=0, keepdims=True))
    decay = jnp.exp(b_last + m_prev - m_new)
    kw = k.astype(F32) * jnp.exp(acol - m_new)
    c_ref[...] = decay * c_ref[...] + lax.dot_general(
        kw.astype(BF16), v, (((0,), (0,)), ((), ())), preferred_element_type=F32)
    n_ref[...] = decay * n_ref[...] + jnp.sum(kw, axis=0, keepdims=True)
    m_ref[...] = m_new
    return out


def _mlstm_body(q_ref, k_ref, v_ref, o_ref, bcol_ref, gcol_ref, grow_ref, hn_ref, out_ref,
                c_ref, n_ref, m_ref, *, chunk):
    @pl.when(pl.program_id(2) == 0)
    def _():
        c_ref[...] = jnp.zeros_like(c_ref)
        n_ref[...] = jnp.zeros_like(n_ref)
        m_ref[...] = jnp.zeros_like(m_ref)

    for e in range(MLSTM_HEADS):
        qk_cols = slice(e * M_DQK, (e + 1) * M_DQK)
        v_cols = slice(e * M_DV, (e + 1) * M_DV)
        out = _mlstm_head(pl.program_id(1) * MLSTM_HEADS + e, q_ref[:, qk_cols], k_ref[:, qk_cols],
                          v_ref[:, v_cols], o_ref[:, v_cols], hn_ref[:, v_cols],
                          bcol_ref[...], gcol_ref[...], grow_ref[...],
                          c_ref.at[e], n_ref.at[e], m_ref.at[e], chunk=chunk)
        out_ref[:, v_cols] = out.astype(out_ref.dtype)


def _mlstm(proj, bcol, gcol, grow, head_gain, *, batch, seq, chunk):
    m = proj.shape[0]
    nc = seq // chunk
    hb = MLSTM_HEADS
    k_blk0 = M_HEADS // hb
    v_blk0 = (2 * M_HEADS * M_DQK) // (hb * M_DV)
    o_blk0 = v_blk0 + M_HEADS // hb
    rows = lambda b, h, c: b * nc + c
    return pl.pallas_call(
        functools.partial(_mlstm_body, chunk=chunk),
        grid=(batch, M_HEADS // hb, nc),
        in_specs=[
            pl.BlockSpec((chunk, hb * M_DQK), lambda b, h, c: (rows(b, h, c), h)),
            pl.BlockSpec((chunk, hb * M_DQK), lambda b, h, c: (rows(b, h, c), k_blk0 + h)),
            pl.BlockSpec((chunk, hb * M_DV), lambda b, h, c: (rows(b, h, c), v_blk0 + h)),
            pl.BlockSpec((chunk, hb * M_DV), lambda b, h, c: (rows(b, h, c), o_blk0 + h)),
            pl.BlockSpec((chunk, V7X_LANES), lambda b, h, c: (rows(b, h, c), 0)),
            pl.BlockSpec((chunk, V7X_LANES), lambda b, h, c: (rows(b, h, c), 0)),
            pl.BlockSpec((V7X_SUBLANES, chunk), lambda b, h, c: (0, rows(b, h, c))),
            pl.BlockSpec((1, hb * M_DV), lambda b, h, c: (0, h)),
        ],
        out_specs=pl.BlockSpec((chunk, hb * M_DV), lambda b, h, c: (rows(b, h, c), h)),
        out_shape=jax.ShapeDtypeStruct((m, M_HEADS * M_DV), BF16),
        scratch_shapes=[pltpu.VMEM((hb, M_DQK, M_DV), F32),
                        pltpu.VMEM((hb, 1, M_DQK), F32),
                        pltpu.VMEM((hb, 1, 1), F32)],
        compiler_params=pltpu.CompilerParams(
            dimension_semantics=("parallel", "parallel", "arbitrary")),
        name="mlstm",
    )(proj, proj, proj, proj, bcol, gcol, grow, head_gain.reshape(1, M_HEADS * M_DV))


def _row_rmsnorm(a, gain):
    ms = jnp.mean(a * a, axis=-1, keepdims=True)
    return (a * lax.rsqrt(ms + EPS) * gain).astype(BF16)


def _qproj_body(a_ref, an_ref, w_ref, gq_ref, cos_ref, sin_lo_ref, sin_hi_ref, o_ref, *, heads_per_block):
    a = _row_rmsnorm(a_ref[...], an_ref[...])
    cos = cos_ref[...]
    sin = sin_lo_ref[...] + sin_hi_ref[...]
    gq = gq_ref[...]
    real = lax.broadcasted_iota(jnp.int32, (1, A_HEAD_PAD), 1) < A_QK
    half = a.shape[0] // 2
    for r in range(2):
        rows = slice(r * half, (r + 1) * half)
        acc = jnp.dot(a[rows], w_ref[...], preferred_element_type=F32)
        for hh in range(heads_per_block):
            blk = acc[:, hh * A_HEAD_PAD:(hh + 1) * A_HEAD_PAD]
            ss = jnp.sum(jnp.where(real, blk * blk, 0.0), axis=-1, keepdims=True)
            qn = blk * lax.rsqrt(ss * (1.0 / A_QK) + EPS) * gq
            rope = qn[:, V7X_LANES:]
            rope = rope * cos[rows] + pltpu.roll(rope, shift=V7X_LANES - ROPE_HALF, axis=1) * sin[rows]
            o_ref[rows, hh * A_HEAD_PAD:hh * A_HEAD_PAD + V7X_LANES] = qn[:, :V7X_LANES].astype(o_ref.dtype)
            o_ref[rows, hh * A_HEAD_PAD + V7X_LANES:(hh + 1) * A_HEAD_PAD] = rope.astype(o_ref.dtype)


def _qproj(small, a_norm, w_q, gq, rope, *, rank, bm=1024, heads_per_block=4):
    m = small.shape[0]
    bn = heads_per_block * A_HEAD_PAD
    n = w_q.shape[1]
    rowvec = pl.BlockSpec((bm, V7X_LANES), lambda i, j: (i, 0))
    return pl.pallas_call(
        functools.partial(_qproj_body, heads_per_block=heads_per_block),
        grid=(m // bm, n // bn),
        in_specs=[pl.BlockSpec((bm, rank), lambda i, j: (i, 0)),
                  pl.BlockSpec((1, rank), lambda i, j: (0, 0)),
                  pl.BlockSpec((rank, bn), lambda i, j: (0, j)),
                  pl.BlockSpec((1, A_HEAD_PAD), lambda i, j: (0, 0)),
                  rowvec, rowvec, rowvec],
        out_specs=pl.BlockSpec((bm, bn), lambda i, j: (i, j)),
        out_shape=jax.ShapeDtypeStruct((m, n), BF16),
        compiler_params=pltpu.CompilerParams(
            dimension_semantics=("parallel", "parallel"),
            vmem_limit_bytes=_vmem_limit([_nbytes((bm, rank), F32), _nbytes((rank, bn), BF16),
                                          _nbytes((bm, bn), BF16)],
                                         temp_bytes=4 * _nbytes((bm, bn), F32))),
        name="mla_qproj",
    )(small, a_norm.reshape(1, rank), w_q, gq, *rope)


def _kvproj_body(a_ref, an_ref, w_ref, kr_ref, gkn_ref, gkr_ref, cos_ref, sin_lo_ref, sin_hi_ref, k_ref, v_ref, *,
                 heads_per_block):
    a = _row_rmsnorm(a_ref[...], an_ref[...])
    kr = kr_ref[...]
    ss_rope = jnp.sum(kr * kr, axis=-1, keepdims=True)
    k_rot = _rotate(kr * gkr_ref[...], (cos_ref[...], sin_lo_ref[...], sin_hi_ref[...]))
    gkn = gkn_ref[...]
    half = a.shape[0] // 2
    ones = jnp.ones((half, A_DV), v_ref.dtype)
    for r in range(2):
        rows = slice(r * half, (r + 1) * half)
        acc = jnp.dot(a[rows], w_ref[...], preferred_element_type=F32)
        for hh in range(heads_per_block):
            kv = acc[:, hh * (A_NOPE + A_DV):(hh + 1) * (A_NOPE + A_DV)]
            kn = kv[:, :A_NOPE]
            ss = jnp.sum(kn * kn, axis=-1, keepdims=True) + ss_rope[rows]
            rs = lax.rsqrt(ss * (1.0 / A_QK) + EPS)
            k_ref[rows, hh * A_HEAD_PAD:hh * A_HEAD_PAD + V7X_LANES] = (kn * rs * gkn).astype(k_ref.dtype)
            k_ref[rows, hh * A_HEAD_PAD + V7X_LANES:(hh + 1) * A_HEAD_PAD] = (k_rot[rows] * rs).astype(k_ref.dtype)
            v_ref[rows, hh * A_V_PAD:hh * A_V_PAD + A_DV] = kv[:, A_NOPE:].astype(v_ref.dtype)
            v_ref[rows, hh * A_V_PAD + A_DV:(hh + 1) * A_V_PAD] = ones


def _kvproj(small, a_norm, w_kv, gkn, gkr, rope, *, a_blk, rank, rope_blk, bm=1024, heads_per_block=4):
    m = small.shape[0]
    bn = heads_per_block * (A_NOPE + A_DV)
    n_blocks = w_kv.shape[1] // bn
    rowvec = pl.BlockSpec((bm, V7X_LANES), lambda i, j: (i, 0))
    lanevec = pl.BlockSpec((1, V7X_LANES), lambda i, j: (0, 0))
    return pl.pallas_call(
        functools.partial(_kvproj_body, heads_per_block=heads_per_block),
        grid=(m // bm, n_blocks),
        in_specs=[pl.BlockSpec((bm, rank), lambda i, j: (i, a_blk)),
                  pl.BlockSpec((1, rank), lambda i, j: (0, 0)),
                  pl.BlockSpec((rank, bn), lambda i, j: (0, j)),
                  pl.BlockSpec((bm, V7X_LANES), lambda i, j: (i, rope_blk)),
                  lanevec, lanevec, rowvec, rowvec, rowvec],
        out_specs=[pl.BlockSpec((bm, heads_per_block * A_HEAD_PAD), lambda i, j: (i, j)),
                   pl.BlockSpec((bm, heads_per_block * A_V_PAD), lambda i, j: (i, j))],
        out_shape=[jax.ShapeDtypeStruct((m, A_HEADS * A_HEAD_PAD), BF16),
                   jax.ShapeDtypeStruct((m, A_HEADS * A_V_PAD), BF16)],
        compiler_params=pltpu.CompilerParams(
            dimension_semantics=("parallel", "parallel"),
            vmem_limit_bytes=_vmem_limit([_nbytes((bm, rank), F32), _nbytes((rank, bn), BF16),
                                          _nbytes((bm, bn), BF16), _nbytes((bm, bn), BF16)],
                                         temp_bytes=4 * _nbytes((bm, bn), F32))),
        name="mla_kvproj",
    )(small, a_norm.reshape(1, rank), w_kv, small, gkn, gkr, *rope)


FLASH_HEADS = 2


def _flash_body(q_ref, k_ref, v_ref, o_ref, s_ref, mx_ref, acc_ref, *, tq, tkc):
    i = pl.program_id(2)
    n_slab = tkc // V7X_LANES
    n_chunks = ((i + 1) * tq + tkc - 1) // tkc
    mx_ref[...] = jnp.full_like(mx_ref, -jnp.inf)
    acc_ref[...] = jnp.zeros_like(acc_ref)

    def scores(c, masked):
        rows = pl.ds(pl.multiple_of(c * tkc, tkc), tkc)
        for e in range(FLASH_HEADS):
            s = lax.dot_general(q_ref[:, e * A_HEAD_PAD:(e + 1) * A_HEAD_PAD],
                                k_ref[rows, e * A_HEAD_PAD:(e + 1) * A_HEAD_PAD],
                                (((1,), (1,)), ((), ())), preferred_element_type=F32)
            if masked:
                row = lax.broadcasted_iota(jnp.int32, (tq, tkc), 0)
                col = lax.broadcasted_iota(jnp.int32, (tq, tkc), 1)
                s = jnp.where(col + c * tkc <= row + i * tq, s, -jnp.inf)
            s_ref[e, c] = s
            mx = mx_ref[e]
            for t in range(n_slab):
                mx = jnp.maximum(mx, s[:, t * V7X_LANES:(t + 1) * V7X_LANES])
            mx_ref[e] = mx

    def unmasked_pair(t, carry):
        scores(2 * t, False)
        scores(2 * t + 1, False)
        return carry

    lax.fori_loop(0, (n_chunks - 1) // 2, unmasked_pair, 0)

    @pl.when(n_chunks % 2 == 0)
    def _():
        scores(n_chunks - 2, False)
        scores(n_chunks - 1, True)

    @pl.when(n_chunks % 2 == 1)
    def _():
        scores(n_chunks - 1, True)

    for e in range(FLASH_HEADS):
        mx_ref[e] = jnp.broadcast_to(jnp.max(mx_ref[e], axis=-1, keepdims=True), (tq, V7X_LANES))

    def probs(c):
        rows = pl.ds(pl.multiple_of(c * tkc, tkc), tkc)
        for e in range(FLASH_HEADS):
            s = s_ref[e, c]
            mx = mx_ref[e]
            p = jnp.concatenate([jnp.exp2(s[:, t * V7X_LANES:(t + 1) * V7X_LANES] - mx) for t in range(n_slab)],
                                axis=-1).astype(BF16)
            acc_ref[e] += jnp.dot(p, v_ref[rows, e * A_V_PAD:(e + 1) * A_V_PAD], preferred_element_type=F32)

    def probs_pair(t, carry):
        probs(2 * t)
        probs(2 * t + 1)
        return carry

    lax.fori_loop(0, n_chunks // 2, probs_pair, 0)

    @pl.when(n_chunks % 2 == 1)
    def _():
        probs(n_chunks - 1)

    for e in range(FLASH_HEADS):
        acc = acc_ref[e]
        o_ref[:, e * A_DV:(e + 1) * A_DV] = (acc[:, :A_DV] / acc[:, A_DV:]).astype(o_ref.dtype)


def _flash_attention(q, k, v1, *, batch, seq, tq=512, tkc=1024):
    assert tkc % tq == 0 and seq % tkc == 0, (tq, tkc, seq)
    m = q.shape[0]
    nq = seq // tq
    hb = FLASH_HEADS
    blocks = [_nbytes((tq, hb * A_HEAD_PAD), BF16), _nbytes((seq, hb * A_HEAD_PAD), BF16),
              _nbytes((seq, hb * A_V_PAD), BF16), _nbytes((tq, hb * A_DV), BF16)]
    scratch_shapes = [pltpu.VMEM((hb, seq // tkc, tq, tkc), F32),
                      pltpu.VMEM((hb, tq, V7X_LANES), F32),
                      pltpu.VMEM((hb, tq, A_V_PAD), F32)]
    scratch = _nbytes((hb, tq, seq), F32) + _nbytes((hb, tq, V7X_LANES), F32) + _nbytes((hb, tq, A_V_PAD), F32)
    return pl.pallas_call(
        functools.partial(_flash_body, tq=tq, tkc=tkc),
        grid=(batch, A_HEADS // hb, nq),
        in_specs=[
            pl.BlockSpec((tq, hb * A_HEAD_PAD), lambda b, h, i: (b * nq + i, h)),
            pl.BlockSpec((seq, hb * A_HEAD_PAD), lambda b, h, i: (b, h)),
            pl.BlockSpec((seq, hb * A_V_PAD), lambda b, h, i: (b, h)),
        ],
        out_specs=pl.BlockSpec((tq, hb * A_DV), lambda b, h, i: (b * nq + i, h)),
        out_shape=jax.ShapeDtypeStruct((m, A_HEADS * A_DV), BF16),
        scratch_shapes=scratch_shapes,
        compiler_params=pltpu.CompilerParams(
            dimension_semantics=("parallel", "parallel", "parallel"),
            vmem_limit_bytes=_vmem_limit(blocks, scratch_bytes=scratch,
                                         temp_bytes=hb * 3 * _nbytes((tq, tkc), F32))),
        name="mla_flash",
    )(q, k, v1)


def _pad_lanes(x, width):
    return jnp.pad(x, [(0, 0)] * (x.ndim - 1) + [(0, width - x.shape[-1])])


def _layer(xf, rope, layer, norm_mix, w_in_bf16_stack, b_igate, b_fgate, m_head_norm, w_out_m_stack, q_a_norm,
           w_uq, kv_a_norm, w_ukv, qk_norm_q, qk_norm_k, w_out_a_stack, w_out_stack, norm_mlp, w_up_stack,
           w_down_bf16_stack, *, batch, seq):
    d_model = xf.shape[1]
    q_lora = w_uq.shape[0]
    kv_lora = w_ukv.shape[0]
    m_qk_w = M_HEADS * M_DQK
    m_v_w = M_HEADS * M_DV
    w_in_cols = lambda lo, hi: lax.slice(w_in_bf16_stack, (layer, 0, lo), (layer + 1, d_model, hi))[0]
    o_i = 2 * m_qk_w + 2 * m_v_w
    o_f = o_i + M_HEADS
    o_cq = o_f + M_HEADS
    o_kva = o_cq + q_lora
    o_ga = o_kva + kv_lora + A_ROPE
    w_gates = w_in_cols(o_ga, o_ga + 2 * d_model)
    w_small = jnp.concatenate([
        _pad_lanes(w_in_cols(o_cq, o_ga), q_lora + kv_lora + V7X_LANES),
        _pad_lanes(w_in_cols(o_i, o_f), V7X_LANES),
        _pad_lanes(w_in_cols(o_f, o_cq), V7X_LANES),
    ], axis=1)
    kv_blk = q_lora // kv_lora
    rope_blk = (q_lora + kv_lora) // V7X_LANES
    i_blk = rope_blk + 1
    f_blk = i_blk + 1

    twice_rope = lambda t: jnp.concatenate([t, t[..., A_NOPE:]], axis=-1)
    w_q = twice_rope(w_uq.astype(BF16).reshape(q_lora, A_HEADS, A_QK)).reshape(q_lora, A_HEADS * A_HEAD_PAD)
    gq = twice_rope(qk_norm_q * (A_QK ** -0.5 * LOG2E)).reshape(1, A_HEAD_PAD)
    hpb = 4
    w_kv = w_ukv.astype(BF16)
    gkn = qk_norm_k[:A_NOPE].reshape(1, V7X_LANES)
    gkr = _pad_lanes(qk_norm_k[A_NOPE:], V7X_LANES).reshape(1, V7X_LANES)

    h = _rmsnorm(xf, norm_mix)
    proj = _matmul(h, w_in_bf16_stack, layer=layer, n_cols=o_i, bm=1024, bn=1024, out_dtype=BF16,
                   name="in_proj_main")
    gab = _matmul(h, w_gates, bm=1024, bn=1024, out_dtype=BF16, name="in_proj_gates")
    small = _matmul(h, w_small, bm=1024, bn=w_small.shape[1] // 3, out_dtype=F32, name="in_proj_small")

    bcol, gcol, grow = _mlstm_gates(small, b_igate, b_fgate, i_blk=i_blk, f_blk=f_blk, chunk=MLSTM_CHUNK)
    hm = _mlstm(proj, bcol, gcol, grow, m_head_norm, batch=batch, seq=seq, chunk=MLSTM_CHUNK)

    qh = _qproj(small, q_a_norm, w_q, gq, rope, rank=q_lora, heads_per_block=hpb)
    kh, vh = _kvproj(small, kv_a_norm, w_kv, gkn, gkr, rope, a_blk=kv_blk, rank=kv_lora,
                     rope_blk=rope_blk, heads_per_block=hpb)
    att = _flash_attention(qh, kh, vh, batch=batch, seq=seq)

    bn = 1024
    t = _matmul_wstat(hm, w_out_m_stack, layer, bm=512, bn=bn, out_dtype=F32,
                      epilogue=_ep_gate, extras=[(gab, 0)], name="out_proj_mlstm")
    merged = _matmul_wstat(att, w_out_a_stack, layer, bm=512, bn=bn, out_dtype=BF16,
                           epilogue=_ep_gate_add, extras=[(gab, d_model // bn), (t, 0)], name="out_proj_mla")
    xf = _matmul_wstat(merged, w_out_stack, layer, bm=512, bn=bn, out_dtype=F32,
                       epilogue=_ep_residual, extras=[(xf, 0)], name="out_proj")

    h2 = _rmsnorm(xf, norm_mlp)
    u = _matmul_wstat(h2, w_up_stack, layer, bm=1024, bn=1024, out_dtype=BF16, epilogue=_ep_relu2, name="mlp_up")
    return _matmul_kacc(u, w_down_bf16_stack, layer, xf, bm=1024, bn=1024, bk=4096, name="mlp_down")


def kernel(x, positions, norm_mix, w_in, b_igate, b_fgate, m_head_norm, w_out_m, q_a_norm, w_uq,
           kv_a_norm, w_ukv, qk_norm_q, qk_norm_k, w_out_a, w_out, norm_mlp, w_up, w_down):
    batch, seq, d_model = x.shape
    depth = w_in.shape[0]
    m = batch * seq
    inv_freq = ROPE_THETA ** (-jnp.arange(0, A_ROPE, 2, dtype=F32) / A_ROPE)
    ang = positions.astype(F32).reshape(m, 1) * inv_freq
    rope = _rope_tables(_pad_lanes(jnp.concatenate([ang, ang], axis=-1), V7X_LANES))
    xf = x.reshape(m, d_model)
    w_in_bf16 = w_in.astype(BF16)
    w_down_bf16 = w_down.astype(BF16)
    for l in range(depth):
        xf = _layer(xf, rope, l, norm_mix[l], w_in_bf16, b_igate[l], b_fgate[l], m_head_norm[l], w_out_m,
                    q_a_norm[l], w_uq[l], kv_a_norm[l], w_ukv[l], qk_norm_q[l], qk_norm_k[l], w_out_a,
                    w_out, norm_mlp[l], w_up, w_down_bf16, batch=batch, seq=seq)
    return xf.reshape(batch, seq, d_model)
```

```python
import functools
import math

import jax
import jax.numpy as jnp
from jax import lax
from jax.experimental import pallas as pl
from jax.experimental.pallas import tpu as pltpu

F32 = jnp.float32
BF16 = jnp.bfloat16

M_HEADS = 8
M_DQK = 256
M_DV = 512
GATE_SOFTCAP = 15.0
A_HEADS = 32
A_NOPE = 128
A_ROPE = 64
A_QK = A_NOPE + A_ROPE
A_DV = 128
ROPE_THETA = 10000.0
EPS = 1e-6

V7X_LANES = 128
V7X_SUBLANES = 8
V7X_VMEM_BYTES = 64 * 1024 * 1024
V7X_VMEM_REQUEST_CAP = V7X_VMEM_BYTES - 6 * 1024 * 1024

A_HEAD_PAD = 2 * V7X_LANES
A_V_PAD = 2 * A_DV
ROPE_HALF = A_ROPE // 2
MLSTM_CHUNK = 256
LOG2E = math.log2(math.e)


def _vmem_limit(block_bytes, scratch_bytes=0, temp_bytes=0):
    need = 2 * sum(block_bytes) + scratch_bytes + temp_bytes + (2 << 20)
    return int(min(max(need, 16 << 20), V7X_VMEM_REQUEST_CAP))


def _nbytes(shape, dtype):
    return math.prod(shape) * jnp.dtype(dtype).itemsize


def _rmsnorm_body(x_ref, g_ref, o_ref):
    x = x_ref[...]
    ms = jnp.mean(x * x, axis=-1, keepdims=True)
    o_ref[...] = (x * lax.rsqrt(ms + EPS) * g_ref[...]).astype(o_ref.dtype)


def _rmsnorm(x, g, *, tm=256):
    m, d = x.shape
    return pl.pallas_call(
        _rmsnorm_body,
        grid=(m // tm,),
        in_specs=[pl.BlockSpec((tm, d), lambda i: (i, 0)),
                  pl.BlockSpec((1, d), lambda i: (0, 0))],
        out_specs=pl.BlockSpec((tm, d), lambda i: (i, 0)),
        out_shape=jax.ShapeDtypeStruct((m, d), BF16),
        compiler_params=pltpu.CompilerParams(
            dimension_semantics=("parallel",),
            vmem_limit_bytes=_vmem_limit([_nbytes((tm, d), F32), _nbytes((tm, d), BF16)],
                                         temp_bytes=2 * _nbytes((tm, d), F32))),
        name="rmsnorm",
    )(x, g.reshape(1, d))


def _ep_identity(acc):
    return acc


def _ep_relu2(acc):
    return jnp.square(jnp.maximum(acc, 0.0))


def _ep_gate(acc, g):
    return jax.nn.sigmoid(g.astype(F32)) * acc


def _ep_gate_add(acc, g, t):
    return t + jax.nn.sigmoid(g.astype(F32)) * acc


def _ep_residual(acc, r):
    return r + acc


def _mm_body(*refs, n_extra, epilogue):
    a_ref, w_ref = refs[0], refs[1]
    extra_refs = refs[2:2 + n_extra]
    o_ref = refs[2 + n_extra]
    acc = jnp.dot(a_ref[...], w_ref[...], preferred_element_type=F32)
    o_ref[...] = epilogue(acc, *[r[...] for r in extra_refs]).astype(o_ref.dtype)


def _matmul(a, w, *, bm, bn, out_dtype, epilogue=_ep_identity, extras=(), layer=None, n_cols=None, name):
    m, kdim = a.shape
    if layer is None:
        n = w.shape[1]
        w_spec = pl.BlockSpec((kdim, bn), lambda i, j: (0, j))
    else:
        n = n_cols
        w_spec = pl.BlockSpec((None, kdim, bn), lambda i, j: (layer, 0, j))
    in_specs = [pl.BlockSpec((bm, kdim), lambda i, j: (i, 0)), w_spec]
    blocks = [_nbytes((bm, kdim), a.dtype), _nbytes((kdim, bn), w.dtype), _nbytes((bm, bn), out_dtype)]
    for arr, off in extras:
        in_specs.append(pl.BlockSpec((bm, bn), lambda i, j, off=off: (i, off + j)))
        blocks.append(_nbytes((bm, bn), arr.dtype))
    return pl.pallas_call(
        functools.partial(_mm_body, n_extra=len(extras), epilogue=epilogue),
        grid=(m // bm, n // bn),
        in_specs=in_specs,
        out_specs=pl.BlockSpec((bm, bn), lambda i, j: (i, j)),
        out_shape=jax.ShapeDtypeStruct((m, n), out_dtype),
        compiler_params=pltpu.CompilerParams(
            dimension_semantics=("parallel", "parallel"),
            vmem_limit_bytes=_vmem_limit(blocks, temp_bytes=3 * _nbytes((bm, bn), F32))),
        name=name,
    )(a, w, *[arr for arr, _ in extras])


def _mm_kacc_body(a_ref, w_ref, r_ref, o_ref):
    @pl.when(pl.program_id(2) == 0)
    def _():
        o_ref[...] = r_ref[...]

    half = a_ref.shape[0] // 2
    for r in range(2):
        rows = slice(r * half, (r + 1) * half)
        o_ref[rows, :] += jnp.dot(a_ref[rows, :], w_ref[...], preferred_element_type=F32)


def _matmul_kacc(a, w_stack, layer, res, *, bm, bn, bk, name):
    m, kdim = a.shape
    n = w_stack.shape[2]
    blocks = [_nbytes((bm, bk), a.dtype), _nbytes((bk, bn), w_stack.dtype), 2 * _nbytes((bm, bn), F32)]
    return pl.pallas_call(
        _mm_kacc_body,
        grid=(m // bm, n // bn, kdim // bk),
        in_specs=[pl.BlockSpec((bm, bk), lambda i, j, k: (i, k)),
                  pl.BlockSpec((None, bk, bn), lambda i, j, k: (layer, k, j)),
                  pl.BlockSpec((bm, bn), lambda i, j, k: (i, j))],
        out_specs=pl.BlockSpec((bm, bn), lambda i, j, k: (i, j)),
        out_shape=jax.ShapeDtypeStruct((m, n), F32),
        compiler_params=pltpu.CompilerParams(
            dimension_semantics=("parallel", "parallel", "arbitrary"),
            vmem_limit_bytes=_vmem_limit(blocks, temp_bytes=2 * _nbytes((bm, bn), F32))),
        name=name,
    )(a, w_stack, res)


def _mm_wstat_body(*refs, n_extra, epilogue, n_col_blocks):
    a_ref, wn_ref = refs[0], refs[1]
    extra_refs = refs[2:2 + n_extra]
    o_ref = refs[2 + n_extra]
    wb_ref = refs[3 + n_extra]
    p = pl.program_id(0)
    i = pl.program_id(1)
    fill = p % 2
    kc = wn_ref.shape[0]

    @pl.when(p < n_col_blocks)
    def _():
        wb_ref[fill, pl.ds(pl.multiple_of(i * kc, kc), kc), :] = wn_ref[...].astype(BF16)

    @pl.when(p > 0)
    def _():
        acc = jnp.dot(a_ref[...], wb_ref[1 - fill], preferred_element_type=F32)
        o_ref[...] = epilogue(acc, *[r[...] for r in extra_refs]).astype(o_ref.dtype)


def _matmul_wstat(a, w_stack, layer, *, bm, bn, out_dtype, epilogue=_ep_identity, extras=(), name):
    m, kdim = a.shape
    n = w_stack.shape[2]
    n_row_blocks = m // bm
    n_col_blocks = n // bn
    kc = kdim // n_row_blocks
    row = lambda p, i: jnp.where(p == 0, 0, i)
    col = lambda p: jnp.maximum(p - 1, 0)
    in_specs = [pl.BlockSpec((bm, kdim), lambda p, i: (row(p, i), 0)),
                pl.BlockSpec((None, kc, bn), lambda p, i: (layer, i, jnp.minimum(p, n_col_blocks - 1)))]
    blocks = [_nbytes((bm, kdim), a.dtype), _nbytes((kc, bn), F32), _nbytes((bm, bn), out_dtype)]
    for arr, off in extras:
        in_specs.append(pl.BlockSpec((bm, bn), lambda p, i, off=off: (row(p, i), off + col(p))))
        blocks.append(_nbytes((bm, bn), arr.dtype))
    return pl.pallas_call(
        functools.partial(_mm_wstat_body, n_extra=len(extras), epilogue=epilogue, n_col_blocks=n_col_blocks),
        grid=(n_col_blocks + 1, n_row_blocks),
        in_specs=in_specs,
        out_specs=pl.BlockSpec((bm, bn), lambda p, i: (row(p, i), col(p))),
        out_shape=jax.ShapeDtypeStruct((m, n), out_dtype),
        scratch_shapes=[pltpu.VMEM((2, kdim, bn), BF16)],
        compiler_params=pltpu.CompilerParams(
            dimension_semantics=("arbitrary", "arbitrary"),
            vmem_limit_bytes=_vmem_limit(blocks, scratch_bytes=2 * _nbytes((kdim, bn), BF16),
                                         temp_bytes=3 * _nbytes((bm, bn), F32))),
        name=name,
    )(a, w_stack, *[arr for arr, _ in extras])


def _rope_body(ang_ref, cmask_ref, lo_sign_ref, hi_sign_ref, cos_ref, sin_lo_ref, sin_hi_ref):
    ang = ang_ref[...]
    sin = jnp.sin(ang)
    cos_ref[...] = jnp.cos(ang) * cmask_ref[...]
    sin_lo_ref[...] = sin * lo_sign_ref[...]
    sin_hi_ref[...] = sin * hi_sign_ref[...]


def _rope_tables(ang128, *, tm=1024):
    m = ang128.shape[0]
    lane = jnp.arange(V7X_LANES)
    lo = (lane < ROPE_HALF).astype(F32).reshape(1, V7X_LANES)
    hi = jnp.logical_and(lane >= ROPE_HALF, lane < A_ROPE).astype(F32).reshape(1, V7X_LANES)
    row = pl.BlockSpec((tm, V7X_LANES), lambda i: (i, 0))
    vec = pl.BlockSpec((1, V7X_LANES), lambda i: (0, 0))
    return pl.pallas_call(
        _rope_body,
        grid=(m // tm,),
        in_specs=[row, vec, vec, vec],
        out_specs=[row, row, row],
        out_shape=[jax.ShapeDtypeStruct((m, V7X_LANES), F32)] * 3,
        compiler_params=pltpu.CompilerParams(dimension_semantics=("parallel",)),
        name="rope_tables",
    )(ang128, lo + hi, -lo, hi)


def _rotate(x, rope):
    cos, sin_lo, sin_hi = rope
    hi_on_lo = pltpu.roll(x, shift=V7X_LANES - ROPE_HALF, axis=1)
    lo_on_hi = pltpu.roll(x, shift=ROPE_HALF, axis=1)
    return x * cos + hi_on_lo * sin_lo + lo_on_hi * sin_hi


def _split_bf16(x):
    hi = x.astype(BF16)
    r1 = x - hi.astype(F32)
    mid = r1.astype(BF16)
    lo = (r1 - mid.astype(F32)).astype(BF16)
    return hi, mid, lo


def _gate_body(i_ref, f_ref, bi_ref, bf_ref, bcol_ref, gcol_ref, grow_ref, *, chunk):
    li = GATE_SOFTCAP * jnp.tanh((i_ref[...] + bi_ref[...]) / GATE_SOFTCAP)
    z = GATE_SOFTCAP * jnp.tanh((f_ref[...] + bf_ref[...]) / GATE_SOFTCAP)
    lf = jnp.minimum(z, 0.0) - jnp.log(1.0 + jnp.exp(-jnp.abs(z)))
    row = lax.broadcasted_iota(jnp.int32, (chunk, chunk), 0)
    col = lax.broadcasted_iota(jnp.int32, (chunk, chunk), 1)
    tril = (col <= row).astype(BF16)
    b = sum(jnp.dot(tril, piece, preferred_element_type=F32) for piece in _split_bf16(lf))
    g = li - b
    bcol_ref[...] = b
    gcol_ref[...] = g
    grow_ref[...] = g.T[:V7X_SUBLANES, :]


def _mlstm_gates(small, b_i, b_f, *, i_blk, f_blk, chunk):
    m = small.shape[0]
    pad = lambda b: jnp.pad(b.astype(F32), (0, V7X_LANES - b.shape[0])).reshape(1, V7X_LANES)
    col = pl.BlockSpec((chunk, V7X_LANES), lambda c: (c, 0))
    vec = pl.BlockSpec((1, V7X_LANES), lambda c: (0, 0))
    return pl.pallas_call(
        functools.partial(_gate_body, chunk=chunk),
        grid=(m // chunk,),
        in_specs=[pl.BlockSpec((chunk, V7X_LANES), lambda c: (c, i_blk)),
                  pl.BlockSpec((chunk, V7X_LANES), lambda c: (c, f_blk)),
                  vec, vec],
        out_specs=[col, col, pl.BlockSpec((V7X_SUBLANES, chunk), lambda c: (0, c))],
        out_shape=[jax.ShapeDtypeStruct((m, V7X_LANES), F32),
                   jax.ShapeDtypeStruct((m, V7X_LANES), F32),
                   jax.ShapeDtypeStruct((V7X_SUBLANES, m), F32)],
        compiler_params=pltpu.CompilerParams(dimension_semantics=("parallel",)),
        name="mlstm_gates",
    )(small, small, pad(b_i), pad(b_f))


MLSTM_HEADS = 2


def _mlstm_head(head, q, k, v, o_gate, gain, bcol_all, gcol_all, grow_all, c_ref, n_ref, m_ref, *, chunk):
    lane_sel = lax.broadcasted_iota(jnp.int32, (1, V7X_LANES), 1) == head
    bcol = jnp.sum(jnp.where(lane_sel, bcol_all, 0.0), axis=-1, keepdims=True)
    gcol = jnp.sum(jnp.where(lane_sel, gcol_all, 0.0), axis=-1, keepdims=True)
    sub_sel = lax.broadcasted_iota(jnp.int32, (V7X_SUBLANES, 1), 0) == head
    grow = jnp.sum(jnp.where(sub_sel, grow_all, 0.0), axis=0, keepdims=True)

    m_prev = m_ref[...]
    row = lax.broadcasted_iota(jnp.int32, (chunk, chunk), 0)
    col = lax.broadcasted_iota(jnp.int32, (chunk, chunk), 1)
    dmat = jnp.where(col <= row, bcol + grow, -jnp.inf)
    inter = bcol + m_prev
    mj = jnp.maximum(inter, jnp.max(dmat, axis=-1, keepdims=True))
    w_inter = jnp.exp(inter - mj)

    scale = M_DQK ** -0.5
    qk = lax.dot_general(q, k, (((1,), (1,)), ((), ())), preferred_element_type=F32) * scale
    p = jnp.exp(dmat - mj) * qk
    cq = jnp.dot(q, c_ref[...].astype(BF16), preferred_element_type=F32) * scale
    num = w_inter * cq + jnp.dot(p.astype(BF16), v, preferred_element_type=F32)
    qn = jnp.sum(q.astype(F32) * n_ref[...], axis=-1, keepdims=True) * scale
    nq = w_inter * qn + jnp.sum(p, axis=-1, keepdims=True)
    hv = num / jnp.maximum(jnp.abs(nq), jnp.exp(-mj))

    ms = jnp.mean(hv * hv, axis=-1, keepdims=True)
    out = hv * lax.rsqrt(ms + EPS) * gain * jax.nn.sigmoid(o_gate.astype(F32))

    b_last = bcol[chunk - 1:chunk, :]
    acol = b_last + gcol
    m_new = jnp.maximum(b_last + m_prev, jnp.max(acol, axis=0, keepdims=True))
    decay = jnp.exp(b_last + m_prev - m_new)
    kw = k.astype(F32) * jnp.exp(acol - m_new)
    c_ref[...] = decay * c_ref[...] + lax.dot_general(
        kw.astype(BF16), v, (((0,), (0,)), ((), ())), preferred_element_type=F32)
    n_ref[...] = decay * n_ref[...] + jnp.sum(kw, axis=0, keepdims=True)
    m_ref[...] = m_new
    return out


def _mlstm_body(q_ref, k_ref, v_ref, o_ref, bcol_ref, gcol_ref, grow_ref, hn_ref, out_ref,
                c_ref, n_ref, m_ref, *, chunk):
    @pl.when(pl.program_id(2) == 0)
    def _():
        c_ref[...] = jnp.zeros_like(c_ref)
        n_ref[...] = jnp.zeros_like(n_ref)
        m_ref[...] = jnp.zeros_like(m_ref)

    for e in range(MLSTM_HEADS):
        qk_cols = slice(e * M_DQK, (e + 1) * M_DQK)
        v_cols = slice(e * M_DV, (e + 1) * M_DV)
        out = _mlstm_head(pl.program_id(1) * MLSTM_HEADS + e, q_ref[:, qk_cols], k_ref[:, qk_cols],
                          v_ref[:, v_cols], o_ref[:, v_cols], hn_ref[:, v_cols],
                          bcol_ref[...], gcol_ref[...], grow_ref[...],
                          c_ref.at[e], n_ref.at[e], m_ref.at[e], chunk=chunk)
        out_ref[:, v_cols] = out.astype(out_ref.dtype)


def _mlstm(proj, bcol, gcol, grow, head_gain, *, batch, seq, chunk):
    m = proj.shape[0]
    nc = seq // chunk
    hb = MLSTM_HEADS
    k_blk0 = M_HEADS // hb
    v_blk0 = (2 * M_HEADS * M_DQK) // (hb * M_DV)
    o_blk0 = v_blk0 + M_HEADS // hb
    rows = lambda b, h, c: b * nc + c
    return pl.pallas_call(
        functools.partial(_mlstm_body, chunk=chunk),
        grid=(batch, M_HEADS // hb, nc),
        in_specs=[
            pl.BlockSpec((chunk, hb * M_DQK), lambda b, h, c: (rows(b, h, c), h)),
            pl.BlockSpec((chunk, hb * M_DQK), lambda b, h, c: (rows(b, h, c), k_blk0 + h)),
            pl.BlockSpec((chunk, hb * M_DV), lambda b, h, c: (rows(b, h, c), v_blk0 + h)),
            pl.BlockSpec((chunk, hb * M_DV), lambda b, h, c: (rows(b, h, c), o_blk0 + h)),
            pl.BlockSpec((chunk, V7X_LANES), lambda b, h, c: (rows(b, h, c), 0)),
            pl.BlockSpec((chunk, V7X_LANES), lambda b, h, c: (rows(b, h, c), 0)),
            pl.BlockSpec((V7X_SUBLANES, chunk), lambda b, h, c: (0, rows(b, h, c))),
            pl.BlockSpec((1, hb * M_DV), lambda b, h, c: (0, h)),
        ],
        out_specs=pl.BlockSpec((chunk, hb * M_DV), lambda b, h, c: (rows(b, h, c), h)),
        out_shape=jax.ShapeDtypeStruct((m, M_HEADS * M_DV), BF16),
        scratch_shapes=[pltpu.VMEM((hb, M_DQK, M_DV), F32),
                        pltpu.VMEM((hb, 1, M_DQK), F32),
                        pltpu.VMEM((hb, 1, 1), F32)],
        compiler_params=pltpu.CompilerParams(
            dimension_semantics=("parallel", "parallel", "arbitrary")),
        name="mlstm",
    )(proj, proj, proj, proj, bcol, gcol, grow, head_gain.reshape(1, M_HEADS * M_DV))


def _row_rmsnorm(a, gain):
    ms = jnp.mean(a * a, axis=-1, keepdims=True)
    return (a * lax.rsqrt(ms + EPS) * gain).astype(BF16)


def _qproj_body(a_ref, an_ref, w_ref, gq_ref, cos_ref, sin_lo_ref, sin_hi_ref, o_ref, *, heads_per_block):
    a = _row_rmsnorm(a_ref[...], an_ref[...])
    cos = cos_ref[...]
    sin = sin_lo_ref[...] + sin_hi_ref[...]
    gq = gq_ref[...]
    real = lax.broadcasted_iota(jnp.int32, (1, A_HEAD_PAD), 1) < A_QK
    half = a.shape[0] // 2
    for r in range(2):
        rows = slice(r * half, (r + 1) * half)
        acc = jnp.dot(a[rows], w_ref[...], preferred_element_type=F32)
        for hh in range(heads_per_block):
            blk = acc[:, hh * A_HEAD_PAD:(hh + 1) * A_HEAD_PAD]
            ss = jnp.sum(jnp.where(real, blk * blk, 0.0), axis=-1, keepdims=True)
            qn = blk * lax.rsqrt(ss * (1.0 / A_QK) + EPS) * gq
            rope = qn[:, V7X_LANES:]
            rope = rope * cos[rows] + pltpu.roll(rope, shift=V7X_LANES - ROPE_HALF, axis=1) * sin[rows]
            o_ref[rows, hh * A_HEAD_PAD:hh * A_HEAD_PAD + V7X_LANES] = qn[:, :V7X_LANES].astype(o_ref.dtype)
            o_ref[rows, hh * A_HEAD_PAD + V7X_LANES:(hh + 1) * A_HEAD_PAD] = rope.astype(o_ref.dtype)


def _qproj(small, a_norm, w_q, gq, rope, *, rank, bm=1024, heads_per_block=4):
    m = small.shape[0]
    bn = heads_per_block * A_HEAD_PAD
    n = w_q.shape[1]
    rowvec = pl.BlockSpec((bm, V7X_LANES), lambda i, j: (i, 0))
    return pl.pallas_call(
        functools.partial(_qproj_body, heads_per_block=heads_per_block),
        grid=(m // bm, n // bn),
        in_specs=[pl.BlockSpec((bm, rank), lambda i, j: (i, 0)),
                  pl.BlockSpec((1, rank), lambda i, j: (0, 0)),
                  pl.BlockSpec((rank, bn), lambda i, j: (0, j)),
                  pl.BlockSpec((1, A_HEAD_PAD), lambda i, j: (0, 0)),
                  rowvec, rowvec, rowvec],
        out_specs=pl.BlockSpec((bm, bn), lambda i, j: (i, j)),
        out_shape=jax.ShapeDtypeStruct((m, n), BF16),
        compiler_params=pltpu.CompilerParams(
            dimension_semantics=("parallel", "parallel"),
            vmem_limit_bytes=_vmem_limit([_nbytes((bm, rank), F32), _nbytes((rank, bn), BF16),
                                          _nbytes((bm, bn), BF16)],
                                         temp_bytes=4 * _nbytes((bm, bn), F32))),
        name="mla_qproj",
    )(small, a_norm.reshape(1, rank), w_q, gq, *rope)


def _kvproj_body(a_ref, an_ref, w_ref, kr_ref, gkn_ref, gkr_ref, cos_ref, sin_lo_ref, sin_hi_ref, k_ref, v_ref, *,
                 heads_per_block):
    a = _row_rmsnorm(a_ref[...], an_ref[...])
    kr = kr_ref[...]
    ss_rope = jnp.sum(kr * kr, axis=-1, keepdims=True)
    k_rot = _rotate(kr * gkr_ref[...], (cos_ref[...], sin_lo_ref[...], sin_hi_ref[...]))
    gkn = gkn_ref[...]
    half = a.shape[0] // 2
    ones = jnp.ones((half, A_DV), v_ref.dtype)
    for r in range(2):
        rows = slice(r * half, (r + 1) * half)
        acc = jnp.dot(a[rows], w_ref[...], preferred_element_type=F32)
        for hh in range(heads_per_block):
            kv = acc[:, hh * (A_NOPE + A_DV):(hh + 1) * (A_NOPE + A_DV)]
            kn = kv[:, :A_NOPE]
            ss = jnp.sum(kn * kn, axis=-1, keepdims=True) + ss_rope[rows]
            rs = lax.rsqrt(ss * (1.0 / A_QK) + EPS)
            k_ref[rows, hh * A_HEAD_PAD:hh * A_HEAD_PAD + V7X_LANES] = (kn * rs * gkn).astype(k_ref.dtype)
            k_ref[rows, hh * A_HEAD_PAD + V7X_LANES:(hh + 1) * A_HEAD_PAD] = (k_rot[rows] * rs).astype(k_ref.dtype)
            v_ref[rows, hh * A_V_PAD:hh * A_V_PAD + A_DV] = kv[:, A_NOPE:].astype(v_ref.dtype)
            v_ref[rows, hh * A_V_PAD + A_DV:(hh + 1) * A_V_PAD] = ones


def _kvproj(small, a_norm, w_kv, gkn, gkr, rope, *, a_blk, rank, rope_blk, bm=1024, heads_per_block=4):
    m = small.shape[0]
    bn = heads_per_block * (A_NOPE + A_DV)
    n_blocks = w_kv.shape[1] // bn
    rowvec = pl.BlockSpec((bm, V7X_LANES), lambda i, j: (i, 0))
    lanevec = pl.BlockSpec((1, V7X_LANES), lambda i, j: (0, 0))
    return pl.pallas_call(
        functools.partial(_kvproj_body, heads_per_block=heads_per_block),
        grid=(m // bm, n_blocks),
        in_specs=[pl.BlockSpec((bm, rank), lambda i, j: (i, a_blk)),
                  pl.BlockSpec((1, rank), lambda i, j: (0, 0)),
                  pl.BlockSpec((rank, bn), lambda i, j: (0, j)),
                  pl.BlockSpec((bm, V7X_LANES), lambda i, j: (i, rope_blk)),
                  lanevec, lanevec, rowvec, rowvec, rowvec],
        out_specs=[pl.BlockSpec((bm, heads_per_block * A_HEAD_PAD), lambda i, j: (i, j)),
                   pl.BlockSpec((bm, heads_per_block * A_V_PAD), lambda i, j: (i, j))],
        out_shape=[jax.ShapeDtypeStruct((m, A_HEADS * A_HEAD_PAD), BF16),
                   jax.ShapeDtypeStruct((m, A_HEADS * A_V_PAD), BF16)],
        compiler_params=pltpu.CompilerParams(
            dimension_semantics=("parallel", "parallel"),
            vmem_limit_bytes=_vmem_limit([_nbytes((bm, rank), F32), _nbytes((rank, bn), BF16),
                                          _nbytes((bm, bn), BF16), _nbytes((bm, bn), BF16)],
                                         temp_bytes=4 * _nbytes((bm, bn), F32))),
        name="mla_kvproj",
    )(small, a_norm.reshape(1, rank), w_kv, small, gkn, gkr, *rope)


FLASH_HEADS = 2


def _flash_body(q_ref, k_ref, v_ref, o_ref, s_ref, mx_ref, acc_ref, *, tq, tkc, max_chunks):
    i = pl.program_id(2)
    n_slab = tkc // V7X_LANES
    n_chunks = ((i + 1) * tq + tkc - 1) // tkc
    mx_ref[...] = jnp.full_like(mx_ref, -jnp.inf)
    acc_ref[...] = jnp.zeros_like(acc_ref)

    def scores(c, masked):
        rows = slice(c * tkc, (c + 1) * tkc)
        for e in range(FLASH_HEADS):
            s = lax.dot_general(q_ref[:, e * A_HEAD_PAD:(e + 1) * A_HEAD_PAD],
                                k_ref[rows, e * A_HEAD_PAD:(e + 1) * A_HEAD_PAD],
                                (((1,), (1,)), ((), ())), preferred_element_type=F32)
            if masked:
                row = lax.broadcasted_iota(jnp.int32, (tq, tkc), 0)
                col = lax.broadcasted_iota(jnp.int32, (tq, tkc), 1)
                s = jnp.where(col + c * tkc <= row + i * tq, s, -jnp.inf)
            s_ref[e, c] = s
            mx = mx_ref[e]
            for t in range(n_slab):
                mx = jnp.maximum(mx, s[:, t * V7X_LANES:(t + 1) * V7X_LANES])
            mx_ref[e] = mx

    for n in range(1, max_chunks + 1):
        @pl.when(n_chunks == n)
        def _():
            for c in range(n - 1):
                scores(c, False)
            scores(n - 1, True)

    for e in range(FLASH_HEADS):
        mx_ref[e] = jnp.broadcast_to(jnp.max(mx_ref[e], axis=-1, keepdims=True), (tq, V7X_LANES))

    def probs(c):
        rows = slice(c * tkc, (c + 1) * tkc)
        for e in range(FLASH_HEADS):
            s = s_ref[e, c]
            mx = mx_ref[e]
            p = jnp.concatenate([jnp.exp2(s[:, t * V7X_LANES:(t + 1) * V7X_LANES] - mx) for t in range(n_slab)],
                                axis=-1).astype(BF16)
            acc_ref[e] += jnp.dot(p, v_ref[rows, e * A_V_PAD:(e + 1) * A_V_PAD], preferred_element_type=F32)

    for n in range(1, max_chunks + 1):
        @pl.when(n_chunks == n)
        def _():
            for c in range(n):
                probs(c)

    for e in range(FLASH_HEADS):
        acc = acc_ref[e]
        o_ref[:, e * A_DV:(e + 1) * A_DV] = (acc[:, :A_DV] / acc[:, A_DV:]).astype(o_ref.dtype)


def _flash_attention(q, k, v1, *, batch, seq, tq=512, tkc=1024):
    assert tkc % tq == 0 and seq % tkc == 0, (tq, tkc, seq)
    m = q.shape[0]
    nq = seq // tq
    hb = FLASH_HEADS
    blocks = [_nbytes((tq, hb * A_HEAD_PAD), BF16), _nbytes((seq, hb * A_HEAD_PAD), BF16),
              _nbytes((seq, hb * A_V_PAD), BF16), _nbytes((tq, hb * A_DV), BF16)]
    scratch_shapes = [pltpu.VMEM((hb, seq // tkc, tq, tkc), F32),
                      pltpu.VMEM((hb, tq, V7X_LANES), F32),
                      pltpu.VMEM((hb, tq, A_V_PAD), F32)]
    scratch = _nbytes((hb, tq, seq), F32) + _nbytes((hb, tq, V7X_LANES), F32) + _nbytes((hb, tq, A_V_PAD), F32)
    return pl.pallas_call(
        functools.partial(_flash_body, tq=tq, tkc=tkc, max_chunks=seq // tkc),
        grid=(batch, A_HEADS // hb, nq),
        in_specs=[
            pl.BlockSpec((tq, hb * A_HEAD_PAD), lambda b, h, i: (b * nq + i, h)),
            pl.BlockSpec((seq, hb * A_HEAD_PAD), lambda b, h, i: (b, h)),
            pl.BlockSpec((seq, hb * A_V_PAD), lambda b, h, i: (b, h)),
        ],
        out_specs=pl.BlockSpec((tq, hb * A_DV), lambda b, h, i: (b * nq + i, h)),
        out_shape=jax.ShapeDtypeStruct((m, A_HEADS * A_DV), BF16),
        scratch_shapes=scratch_shapes,
        compiler_params=pltpu.CompilerParams(
            dimension_semantics=("parallel", "parallel", "parallel"),
            vmem_limit_bytes=_vmem_limit(blocks, scratch_bytes=scratch,
                                         temp_bytes=hb * 3 * _nbytes((tq, tkc), F32))),
        name="mla_flash",
    )(q, k, v1)


def _pad_lanes(x, width):
    return jnp.pad(x, [(0, 0)] * (x.ndim - 1) + [(0, width - x.shape[-1])])


def _layer(xf, rope, layer, norm_mix, w_in_bf16_stack, b_igate, b_fgate, m_head_norm, w_out_m_stack, q_a_norm,
           w_uq, kv_a_norm, w_ukv, qk_norm_q, qk_norm_k, w_out_a_stack, w_out_stack, norm_mlp, w_up_stack,
           w_down_bf16_stack, *, batch, seq):
    d_model = xf.shape[1]
    q_lora = w_uq.shape[0]
    kv_lora = w_ukv.shape[0]
    m_qk_w = M_HEADS * M_DQK
    m_v_w = M_HEADS * M_DV
    w_in_cols = lambda lo, hi: lax.slice(w_in_bf16_stack, (layer, 0, lo), (layer + 1, d_model, hi))[0]
    o_i = 2 * m_qk_w + 2 * m_v_w
    o_f = o_i + M_HEADS
    o_cq = o_f + M_HEADS
    o_kva = o_cq + q_lora
    o_ga = o_kva + kv_lora + A_ROPE
    w_gates = w_in_cols(o_ga, o_ga + 2 * d_model)
    w_small = jnp.concatenate([
        _pad_lanes(w_in_cols(o_cq, o_ga), q_lora + kv_lora + V7X_LANES),
        _pad_lanes(w_in_cols(o_i, o_f), V7X_LANES),
        _pad_lanes(w_in_cols(o_f, o_cq), V7X_LANES),
    ], axis=1)
    kv_blk = q_lora // kv_lora
    rope_blk = (q_lora + kv_lora) // V7X_LANES
    i_blk = rope_blk + 1
    f_blk = i_blk + 1

    twice_rope = lambda t: jnp.concatenate([t, t[..., A_NOPE:]], axis=-1)
    w_q = twice_rope(w_uq.astype(BF16).reshape(q_lora, A_HEADS, A_QK)).reshape(q_lora, A_HEADS * A_HEAD_PAD)
    gq = twice_rope(qk_norm_q * (A_QK ** -0.5 * LOG2E)).reshape(1, A_HEAD_PAD)
    hpb = 4
    w_kv = w_ukv.astype(BF16)
    gkn = qk_norm_k[:A_NOPE].reshape(1, V7X_LANES)
    gkr = _pad_lanes(qk_norm_k[A_NOPE:], V7X_LANES).reshape(1, V7X_LANES)

    h = _rmsnorm(xf, norm_mix)
    proj = _matmul(h, w_in_bf16_stack, layer=layer, n_cols=o_i, bm=1024, bn=1024, out_dtype=BF16,
                   name="in_proj_main")
    gab = _matmul(h, w_gates, bm=1024, bn=1024, out_dtype=BF16, name="in_proj_gates")
    small = _matmul(h, w_small, bm=1024, bn=w_small.shape[1] // 3, out_dtype=F32, name="in_proj_small")

    bcol, gcol, grow = _mlstm_gates(small, b_igate, b_fgate, i_blk=i_blk, f_blk=f_blk, chunk=MLSTM_CHUNK)
    hm = _mlstm(proj, bcol, gcol, grow, m_head_norm, batch=batch, seq=seq, chunk=MLSTM_CHUNK)

    qh = _qproj(small, q_a_norm, w_q, gq, rope, rank=q_lora, heads_per_block=hpb)
    kh, vh = _kvproj(small, kv_a_norm, w_kv, gkn, gkr, rope, a_blk=kv_blk, rank=kv_lora,
                     rope_blk=rope_blk, heads_per_block=hpb)
    att = _flash_attention(qh, kh, vh, batch=batch, seq=seq)

    bn = 1024
    t = _matmul_wstat(hm, w_out_m_stack, layer, bm=512, bn=bn, out_dtype=F32,
                      epilogue=_ep_gate, extras=[(gab, 0)], name="out_proj_mlstm")
    merged = _matmul_wstat(att, w_out_a_stack, layer, bm=512, bn=bn, out_dtype=BF16,
                           epilogue=_ep_gate_add, extras=[(gab, d_model // bn), (t, 0)], name="out_proj_mla")
    xf = _matmul_wstat(merged, w_out_stack, layer, bm=512, bn=bn, out_dtype=F32,
                       epilogue=_ep_residual, extras=[(xf, 0)], name="out_proj")

    h2 = _rmsnorm(xf, norm_mlp)
    u = _matmul_wstat(h2, w_up_stack, layer, bm=1024, bn=1024, out_dtype=BF16, epilogue=_ep_relu2, name="mlp_up")
    return _matmul_kacc(u, w_down_bf16_stack, layer, xf, bm=1024, bn=1024, bk=4096, name="mlp_down")


def kernel(x, positions, norm_mix, w_in, b_igate, b_fgate, m_head_norm, w_out_m, q_a_norm, w_uq,
           kv_a_norm, w_ukv, qk_norm_q, qk_norm_k, w_out_a, w_out, norm_mlp, w_up, w_down):
    batch, seq, d_model = x.shape
    depth = w_in.shape[0]
    m = batch * seq
    inv_freq = ROPE_THETA ** (-jnp.arange(0, A_ROPE, 2, dtype=F32) / A_ROPE)
    ang = positions.astype(F32).reshape(m, 1) * inv_freq
    rope = _rope_tables(_pad_lanes(jnp.concatenate([ang, ang], axis=-1), V7X_LANES))
    xf = x.reshape(m, d_model)
    w_in_bf16 = w_in.astype(BF16)
    w_down_bf16 = w_down.astype(BF16)
    for l in range(depth):
        xf = _layer(xf, rope, l, norm_mix[l], w_in_bf16, b_igate[l], b_fgate[l], m_head_norm[l], w_out_m,
                    q_a_norm[l], w_uq[l], kv_a_norm[l], w_ukv[l], qk_norm_q[l], qk_norm_k[l], w_out_a,
                    w_out, norm_mlp[l], w_up, w_down_bf16, batch=batch, seq=seq)
    return xf.reshape(batch, seq, d_model)
```
